```python
import math
import jax, jax.numpy as jnp
from jax import lax
import numpy as np

D_MODEL = 1024
BATCH = 8
SEQ = 8192
DEPTH = 4

HEAD_DIM = 64
A_WIDTH = D_MODEL // 2
A_HEADS = A_WIDTH // HEAD_DIM
B_WIDTH = D_MODEL // 4
B_HEADS = B_WIDTH // HEAD_DIM
C_WIDTH = D_MODEL // 4
MIX_WIDTH = A_WIDTH + B_WIDTH + C_WIDTH
IN_WIDTH = 3 * A_WIDTH + 3 * B_WIDTH + C_WIDTH
DILATED_CONFIGS = ((128, 1), (512, 4), (2048, 16))
GRID_W = 64
NA_ROWS = 8
NA_COLS = 16
NA_QC = 16
NA_KC = NA_QC + NA_COLS
S5_GROUP = 16
S5_GROUPS = C_WIDTH // S5_GROUP
S5_STATE = 64
S5_DT_MIN = 1e-3
S5_DT_MAX = 1e-1
N_EXPERTS = 16
EC_CAPACITY = 2
EXPERT_FF = 2 * D_MODEL
EPS = 1e-6
NEG_INF = -1e30

kernel_name = "hybrid_dilated_natten_s5_ec_encoder"


def rms_norm(x, g):
    xf = x.astype(jnp.float32)
    y = xf * lax.rsqrt(jnp.mean(xf * xf, axis=-1, keepdims=True) + EPS)
    return y.astype(x.dtype) * g


def split_heads(t, n):
    b, s, _ = t.shape
    return t.reshape(b, s, n, HEAD_DIM).transpose(0, 2, 1, 3)


def merge_heads(t):
    b, h, s, d = t.shape
    return t.transpose(0, 2, 1, 3).reshape(b, s, h * d)


def alibi_slopes(n):
    return np.array([2.0 ** (-8.0 * (i + 1) / n) for i in range(n)], dtype=np.float32)


def dilated_branch(q, k, v, window, dil, slopes):
    b, h, s, dh = q.shape
    half = window // (2 * dil)
    L = s // dil
    nb = -(-L // half)
    lp = nb * half

    def classes(t):
        return t.reshape(b, h, L, dil, dh).transpose(0, 1, 3, 2, 4)

    qb = jnp.pad(classes(q), ((0, 0), (0, 0), (0, 0), (0, lp - L), (0, 0))).reshape(b, h, dil, nb, half, dh)

    def band(t):
        t = jnp.pad(classes(t), ((0, 0), (0, 0), (0, 0), (half, lp - L + half), (0, 0)))
        t = t.reshape(b, h, dil, nb + 2, half, dh)
        return jnp.concatenate([t[:, :, :, :-2], t[:, :, :, 1:-1], t[:, :, :, 2:]], axis=4)

    kb, vb = band(k), band(v)
    rel = np.arange(3 * half)[None, :] - half - np.arange(half)[:, None]
    kpos = np.arange(nb)[:, None, None] * half + np.arange(half)[None, :, None] + rel[None]
    valid = (np.abs(rel) <= half)[None] & (kpos >= 0) & (kpos < L)
    bias = -(slopes[:, None, None] * (dil * np.abs(rel)).astype(np.float32)[None])
    sc = jnp.einsum('bhrnqd,bhrnkd->bhrnqk', qb, kb).astype(jnp.float32) * (HEAD_DIM ** -0.5)
    sc = jnp.where(valid, sc + bias[:, None, None], NEG_INF)
    m = jnp.max(sc, axis=-1, keepdims=True)
    p = jnp.exp(sc - m)
    den = jnp.sum(p, axis=-1, keepdims=True)
    o = (jnp.einsum('bhrnqk,bhrnkd->bhrnqd', p.astype(v.dtype), vb).astype(jnp.float32) / den).astype(v.dtype)
    lse = (m + jnp.log(den))[..., 0]
    o = o.reshape(b, h, dil, lp, dh)[:, :, :, :L].transpose(0, 1, 3, 2, 4).reshape(b, h, s, dh)
    lse = lse.reshape(b, h, dil, lp)[..., :L].transpose(0, 1, 3, 2).reshape(b, h, s)
    return o, lse


def dilated_attention(q, k, v):
    slopes = alibi_slopes(q.shape[1])
    outs, lses = [], []
    for window, dil in DILATED_CONFIGS:
        o, lse = dilated_branch(q, k, v, window, dil, slopes)
        outs.append(o)
        lses.append(lse)
    w = jax.nn.softmax(jnp.stack(lses, axis=0), axis=0)
    return jnp.sum(w[..., None].astype(q.dtype) * jnp.stack(outs, axis=0), axis=0)


def neighborhood_attention(q, k, v, rpb):
    b, h, s, dh = q.shape
    rows = s // GRID_W
    kr = min(NA_ROWS, rows)
    ncb = GRID_W // NA_QC
    kk = kr * NA_KC
    r = np.arange(rows)
    rs = np.clip(r - NA_ROWS // 2, 0, rows - kr)
    key_row = rs[:, None] + np.arange(kr)[None]
    kcs = np.clip(np.arange(ncb) * NA_QC - NA_COLS // 2, 0, GRID_W - NA_KC)
    key_col = kcs[:, None] + np.arange(NA_KC)[None]
    idx = (key_row[:, None, :, None] * GRID_W + key_col[None, :, None, :]).reshape(-1)
    kg = jnp.take(k, jnp.asarray(idx, dtype=jnp.int32), axis=2).reshape(b, h, rows, ncb, kk, dh)
    vg = jnp.take(v, jnp.asarray(idx, dtype=jnp.int32), axis=2).reshape(b, h, rows, ncb, kk, dh)
    qb = q.reshape(b, h, rows, ncb, NA_QC, dh)
    qcol = np.arange(ncb)[:, None] * NA_QC + np.arange(NA_QC)[None]
    cs = np.clip(qcol - NA_COLS // 2, 0, GRID_W - NA_COLS)
    kcol = np.broadcast_to(key_col[:, None, :], (ncb, kr, NA_KC)).reshape(ncb, kk)
    valid = (kcol[:, None, :] >= cs[:, :, None]) & (kcol[:, None, :] < cs[:, :, None] + NA_COLS)
    krow = np.broadcast_to(key_row[:, :, None], (rows, kr, NA_KC)).reshape(rows, kk)
    dr_idx = (krow - r[:, None] + NA_ROWS - 1)[:, None, None, :]
    dc_idx = np.clip(kcol[:, None, :] - qcol[:, :, None] + NA_COLS - 1, 0, 2 * NA_COLS - 2)[None]
    bias = rpb.astype(jnp.float32)[:, dr_idx, dc_idx]
    sc = jnp.einsum('bhrjqd,bhrjkd->bhrjqk', qb, kg).astype(jnp.float32) * (HEAD_DIM ** -0.5) + bias
    p = jax.nn.softmax(jnp.where(valid, sc, NEG_INF), axis=-1)
    o = jnp.einsum('bhrjqk,bhrjkd->bhrjqd', p.astype(v.dtype), vg)
    return o.reshape(b, h, s, dh)


def complex_affine_combine(e1, e2):
    a1r, a1i, b1r, b1i = e1
    a2r, a2i, b2r, b2i = e2
    return (a2r * a1r - a2i * a1i,
            a2r * a1i + a2i * a1r,
            a2r * b1r - a2i * b1i + b2r,
            a2r * b1i + a2i * b1r + b2i)


def s5_scan(u, a_re, a_im, log_dt, b_re, b_im, c_re, c_im):
    s = u.shape[1]
    dt = jnp.exp(log_dt.astype(jnp.float32))[:, None]
    a = jnp.minimum(a_re.astype(jnp.float32), -1e-4)
    w = a_im.astype(jnp.float32)
    mag = jnp.exp(a * dt)
    abar_r, abar_i = mag * jnp.cos(w * dt), mag * jnp.sin(w * dt)
    den = a * a + w * w
    zr = abar_r - 1.0
    gr = (zr * a + abar_i * w) / den
    gi = (abar_i * a - zr * w) / den
    br, bi = b_re.astype(jnp.float32), b_im.astype(jnp.float32)
    bbar_r = gr[..., None] * br - gi[..., None] * bi
    bbar_i = gr[..., None] * bi + gi[..., None] * br
    bu_r = jnp.einsum('bsgc,gpc->bsgp', u, bbar_r)
    bu_i = jnp.einsum('bsgc,gpc->bsgp', u, bbar_i)
    ar_seq = jnp.broadcast_to(abar_r, (1, s) + abar_r.shape)
    ai_seq = jnp.broadcast_to(abar_i, (1, s) + abar_i.shape)
    _, _, xr, xi = lax.associative_scan(complex_affine_combine, (ar_seq, ai_seq, bu_r, bu_i), axis=1)
    return (jnp.einsum('bsgp,gcp->bsgc', xr, c_re.astype(jnp.float32))
            - jnp.einsum('bsgp,gcp->bsgc', xi, c_im.astype(jnp.float32)))


def s5_mixer(u, a_re, a_im, log_dt, b_re, b_im, c_re, c_im, d_skip, w_glu):
    bsz, s, _ = u.shape
    uf = u.astype(jnp.float32).reshape(bsz, s, S5_GROUPS, S5_GROUP)
    y_f = s5_scan(uf, a_re[0], a_im[0], log_dt[0], b_re[0], b_im[0], c_re[0], c_im[0])
    y_b = jnp.flip(s5_scan(jnp.flip(uf, axis=1), a_re[1], a_im[1], log_dt[1],
                           b_re[1], b_im[1], c_re[1], c_im[1]), axis=1)
    y = (y_f + y_b).reshape(bsz, s, C_WIDTH) + d_skip.astype(jnp.float32) * uf.reshape(bsz, s, C_WIDTH)
    g = jax.nn.gelu(y).astype(u.dtype)
    return g * jax.nn.sigmoid(g @ w_glu)


def ec_moe(h, w_router, w_gate, w_up, w_down):
    b, s, d = h.shape
    cap = EC_CAPACITY * s // N_EXPERTS
    aff = jax.nn.softmax(jnp.einsum('bsd,de->bse', h, w_router).astype(jnp.float32), axis=-1)
    gate, idx = lax.top_k(aff.transpose(0, 2, 1), cap)
    xe = jax.vmap(lambda hb, ib: hb[ib])(h, idx)
    hid = jax.nn.silu(jnp.einsum('becd,edf->becf', xe, w_gate)) * jnp.einsum('becd,edf->becf', xe, w_up)
    ye = jnp.einsum('becf,efd->becd', hid, w_down) * gate[..., None].astype(h.dtype)
    return jax.vmap(lambda ib, yb: jnp.zeros((s, d), yb.dtype).at[ib.reshape(-1)].add(yb.reshape(-1, d)))(idx, ye)


def setup_inputs(seed: int = 0) -> dict:
    key = jax.random.key(seed)
    ks = jax.random.split(key, 26)

    def nrm(k, shape, scale):
        return scale * jax.random.normal(k, shape, jnp.float32)

    def gain(k, shape):
        return 1.0 + 0.02 * jax.random.normal(k, shape, jnp.float32)

    sh5 = (DEPTH, 2, S5_GROUPS, S5_STATE)
    return {
        "x": nrm(ks[0], (BATCH, SEQ, D_MODEL), 1.0),
        "attn_norm": gain(ks[1], (DEPTH, D_MODEL)),
        "w_in": nrm(ks[2], (DEPTH, D_MODEL, IN_WIDTH), D_MODEL ** -0.5),
        "q_norm_a": gain(ks[3], (DEPTH, HEAD_DIM)),
        "k_norm_a": gain(ks[4], (DEPTH, HEAD_DIM)),
        "q_norm_b": gain(ks[5], (DEPTH, HEAD_DIM)),
        "k_norm_b": gain(ks[6], (DEPTH, HEAD_DIM)),
        "rel_pos_bias": nrm(ks[7], (DEPTH, B_HEADS, 2 * NA_ROWS - 1, 2 * NA_COLS - 1), 0.1),
        "s5_a_re": -0.5 * jnp.exp(nrm(ks[8], sh5, 0.05)),
        "s5_a_im": math.pi * jnp.arange(S5_STATE, dtype=jnp.float32) + nrm(ks[9], sh5, 0.01),
        "s5_log_dt": jax.random.uniform(ks[10], (DEPTH, 2, S5_GROUPS), jnp.float32,
                                        math.log(S5_DT_MIN), math.log(S5_DT_MAX)),
        "s5_b_re": nrm(ks[11], (DEPTH, 2, S5_GROUPS, S5_STATE, S5_GROUP), (2 * S5_GROUP) ** -0.5),
        "s5_b_im": nrm(ks[12], (DEPTH, 2, S5_GROUPS, S5_STATE, S5_GROUP), (2 * S5_GROUP) ** -0.5),
        "s5_c_re": nrm(ks[13], (DEPTH, 2, S5_GROUPS, S5_GROUP, S5_STATE), 0.5),
        "s5_c_im": nrm(ks[14], (DEPTH, 2, S5_GROUPS, S5_GROUP, S5_STATE), 0.5),
        "s5_d": nrm(ks[15], (DEPTH, C_WIDTH), 0.5),
        "w_glu": nrm(ks[16], (DEPTH, C_WIDTH, C_WIDTH), C_WIDTH ** -0.5),
        "out_norm_a": gain(ks[17], (DEPTH, A_WIDTH)),
        "out_norm_b": gain(ks[18], (DEPTH, B_WIDTH)),
        "out_norm_c": gain(ks[19], (DEPTH, C_WIDTH)),
        "w_out": nrm(ks[20], (DEPTH, MIX_WIDTH, D_MODEL), MIX_WIDTH ** -0.5),
        "ffn_norm": gain(ks[21], (DEPTH, D_MODEL)),
        "w_router": nrm(ks[22], (DEPTH, D_MODEL, N_EXPERTS), D_MODEL ** -0.5),
        "w_gate": nrm(ks[23], (DEPTH, N_EXPERTS, D_MODEL, EXPERT_FF), D_MODEL ** -0.5),
        "w_up": nrm(ks[24], (DEPTH, N_EXPERTS, D_MODEL, EXPERT_FF), D_MODEL ** -0.5),
        "w_down": nrm(ks[25], (DEPTH, N_EXPERTS, EXPERT_FF, D_MODEL), EXPERT_FF ** -0.5),
    }


def reference(x, attn_norm, w_in, q_norm_a, k_norm_a, q_norm_b, k_norm_b, rel_pos_bias,
              s5_a_re, s5_a_im, s5_log_dt, s5_b_re, s5_b_im, s5_c_re, s5_c_im, s5_d, w_glu,
              out_norm_a, out_norm_b, out_norm_c, w_out, ffn_norm, w_router, w_gate, w_up, w_down):
    splits = np.cumsum([A_WIDTH, A_WIDTH, A_WIDTH, B_WIDTH, B_WIDTH, B_WIDTH]).tolist()
    for l in range(DEPTH):
        h = rms_norm(x, attn_norm[l])
        proj = h @ w_in[l]
        qa, ka, va, qb, kb, vb, u = jnp.split(proj, splits, axis=-1)
        qa = rms_norm(split_heads(qa, A_HEADS), q_norm_a[l])
        ka = rms_norm(split_heads(ka, A_HEADS), k_norm_a[l])
        oa = merge_heads(dilated_attention(qa, ka, split_heads(va, A_HEADS)))
        qb = rms_norm(split_heads(qb, B_HEADS), q_norm_b[l])
        kb = rms_norm(split_heads(kb, B_HEADS), k_norm_b[l])
        ob = merge_heads(neighborhood_attention(qb, kb, split_heads(vb, B_HEADS), rel_pos_bias[l]))
        oc = s5_mixer(u, s5_a_re[l], s5_a_im[l], s5_log_dt[l], s5_b_re[l], s5_b_im[l],
                      s5_c_re[l], s5_c_im[l], s5_d[l], w_glu[l])
        mix = jnp.concatenate([rms_norm(oa, out_norm_a[l]), rms_norm(ob, out_norm_b[l]),
                               rms_norm(oc, out_norm_c[l])], axis=-1)
        x = x + mix @ w_out[l]
        x = x + ec_moe(rms_norm(x, ffn_norm[l]), w_router[l], w_gate[l], w_up[l], w_down[l])
    return x
```

```python
import functools
import math

import numpy as np
import jax
import jax.numpy as jnp
from jax import lax
from jax.experimental import pallas as pl
from jax.experimental.pallas import tpu as pltpu

HEAD_DIM = 64
A_HEADS = 8
B_HEADS = 4
DILATED_CONFIGS = ((128, 1), (512, 4), (2048, 16))
GRID_W = 64
NA_ROWS = 8
NA_COLS = 16
S5_GROUP = 16
S5_STATE = 64
N_EXPERTS = 16
EC_CAPACITY = 2
EPS = 1e-6
NEG_INF = -1e30

LANES = 128
SUBLANES = 8
VMEM_LIMIT = 56 * 1024 * 1024

HALF = 64
QBLK = 2 * HALF
KBLK = 4 * HALF
DIL_CHUNK = 2048

F32 = jnp.float32
BF16 = jnp.bfloat16

_NT = (((1,), (1,)), ((), ()))


def _params(sem, vmem=VMEM_LIMIT):
    return pltpu.CompilerParams(dimension_semantics=sem, vmem_limit_bytes=vmem)


def _lo_lanes():
    return lax.broadcasted_iota(jnp.int32, (1, LANES), 1) < HEAD_DIM


def _stack_heads(q2, lo):
    zero = jnp.zeros_like(q2)
    return jnp.concatenate([jnp.where(lo, q2, zero), jnp.where(lo, zero, q2)], axis=0)


def _in_proj_kernel(x_ref, g_ref, w_ref, hg_ref, qa_ref, ka_ref, va_ref, qb_ref, kb_ref, vb_ref, u_ref,
                    *, a_pairs, b_pairs):
    x = x_ref[0]
    ms = jnp.mean(x * x, axis=-1, keepdims=True)
    h = (x * lax.rsqrt(ms + EPS)) * g_ref[...]
    proj = jnp.dot(h.astype(BF16), w_ref[...], preferred_element_type=F32)
    lo = _lo_lanes()

    def head_norm(c, gain):
        sq = c * c
        s_lo = jnp.sum(jnp.where(lo, sq, 0.0), axis=-1, keepdims=True)
        s_hi = jnp.sum(jnp.where(lo, 0.0, sq), axis=-1, keepdims=True)
        r = jnp.where(lo, lax.rsqrt(s_lo / HEAD_DIM + EPS), lax.rsqrt(s_hi / HEAD_DIM + EPS))
        return ((c * r) * gain).astype(BF16)

    col = 0
    for ref, pairs, gain_row in ((qa_ref, a_pairs, 0), (ka_ref, a_pairs, 1), (va_ref, a_pairs, None),
                                 (qb_ref, b_pairs, 2), (kb_ref, b_pairs, 3), (vb_ref, b_pairs, None)):
        for p in range(pairs):
            c = proj[:, col:col + LANES]
            if gain_row is None:
                ref[0, p] = c.astype(BF16)
            else:
                ref[0, p] = head_norm(c, hg_ref[gain_row:gain_row + 1, :])
            col += LANES
    u_ref[...] = proj[:, col:]


def _in_proj(x, g, w, hg, tm):
    b, s, d = x.shape
    a_pairs, b_pairs = A_HEADS // 2, B_HEADS // 2
    cw = w.shape[1] - 3 * (a_pairs + b_pairs) * LANES
    qkv_a = jax.ShapeDtypeStruct((b, a_pairs, s, LANES), BF16)
    qkv_b = jax.ShapeDtypeStruct((b, b_pairs, s, LANES), BF16)
    spec_a = pl.BlockSpec((1, a_pairs, tm, LANES), lambda bi, i: (bi, 0, i, 0))
    spec_b = pl.BlockSpec((1, b_pairs, tm, LANES), lambda bi, i: (bi, 0, i, 0))
    return pl.pallas_call(
        functools.partial(_in_proj_kernel, a_pairs=a_pairs, b_pairs=b_pairs),
        grid=(b, s // tm),
        in_specs=[pl.BlockSpec((1, tm, d), lambda bi, i: (bi, i, 0)),
                  pl.BlockSpec((1, d), lambda bi, i: (0, 0)),
                  pl.BlockSpec(w.shape, lambda bi, i: (0, 0)),
                  pl.BlockSpec(hg.shape, lambda bi, i: (0, 0))],
        out_specs=[spec_a, spec_a, spec_a, spec_b, spec_b, spec_b,
                   pl.BlockSpec((tm, cw), lambda bi, i: (i, bi))],
        out_shape=[qkv_a, qkv_a, qkv_a, qkv_b, qkv_b, qkv_b,
                   jax.ShapeDtypeStruct((s, b * cw), F32)],
        compiler_params=_params(("parallel", "parallel")),
        name="in_proj",
    )(x, g, w, hg)


def _dilated_kernel(q_ref, kp_ref, k_ref, kn_ref, vp_ref, v_ref, vn_ref, bias_ref, o_ref, lse_ref,
                    *, dil, rows):
    c = pl.program_id(2)
    nc = pl.num_programs(2)
    nblk = rows // QBLK
    lo = _lo_lanes()
    kcol = lax.broadcasted_iota(jnp.int32, (1, KBLK), 1)
    edge_first = jnp.where((kcol < HALF) & (c == 0), NEG_INF, 0.0)
    edge_last = jnp.where((kcol >= KBLK - HALF) & (c == nc - 1), NEG_INF, 0.0)
    bias = bias_ref[0]
    for r in range(dil):
        cs = slice(r * LANES, (r + 1) * LANES)
        kc = jnp.concatenate([kp_ref[0, 0, :, cs], k_ref[0, 0, :, cs], kn_ref[0, 0, :, cs]], axis=0)
        vc = jnp.concatenate([vp_ref[0, 0, :, cs], v_ref[0, 0, :, cs], vn_ref[0, 0, :, cs]], axis=0)
        for m in range(nblk):
            rs = slice(m * QBLK, (m + 1) * QBLK)
            qs = _stack_heads(q_ref[0, 0, rs, cs], lo)
            kk = kc[m * QBLK:m * QBLK + KBLK]
            vv = vc[m * QBLK:m * QBLK + KBLK]
            s = lax.dot_general(qs, kk, _NT, preferred_element_type=F32) + bias
            if m == 0:
                s = s + edge_first
            if m == nblk - 1:
                s = s + edge_last
            mx = jnp.max(s, axis=-1, keepdims=True)
            p = jnp.exp(s - mx)
            den = jnp.sum(p, axis=-1, keepdims=True)
            pv = jnp.dot(p.astype(BF16), vv, preferred_element_type=F32)
            o = pv / den
            lse = mx + jnp.log(den)
            o_ref[0, 0, rs, cs] = jnp.where(lo, o[:QBLK], o[QBLK:]).astype(BF16)
            lse_ref[0, 0, rs, cs] = jnp.where(lo, lse[:QBLK], lse[QBLK:])


def _alibi_slopes(n):
    return np.array([2.0 ** (-8.0 * (i + 1) / n) for i in range(n)], dtype=np.float32)


def _dilated_bias(dil):
    rel = (np.arange(KBLK)[None, :] - HALF) - np.arange(QBLK)[:, None]
    valid = np.abs(rel) <= HALF
    dist = (dil * np.abs(rel)).astype(np.float32)
    slopes = _alibi_slopes(A_HEADS)
    per_head = np.where(valid[None], -(slopes[:, None, None] * dist[None]), np.float32(NEG_INF)).astype(np.float32)
    return per_head.reshape(A_HEADS // 2, 2 * QBLK, KBLK)


def _dilated_branch(q, k, v, dil):
    b, hp, s, _ = q.shape
    length = s // dil
    rows = min(DIL_CHUNK // dil, length)
    assert length % rows == 0 and rows % QBLK == 0
    width = dil * LANES
    qv, kv, vv = (t.reshape(b, hp, length, width) for t in (q, k, v))
    hb = rows // HALF
    last = length // HALF - 1
    main = pl.BlockSpec((1, 1, rows, width), lambda bi, h, c: (bi, h, c, 0))
    prev = pl.BlockSpec((1, 1, HALF, width), lambda bi, h, c: (bi, h, jnp.maximum(c * hb - 1, 0), 0))
    nxt = pl.BlockSpec((1, 1, HALF, width), lambda bi, h, c: (bi, h, jnp.minimum((c + 1) * hb, last), 0))
    bias = jnp.asarray(_dilated_bias(dil))
    o, lse = pl.pallas_call(
        functools.partial(_dilated_kernel, dil=dil, rows=rows),
        grid=(b, hp, length // rows),
        in_specs=[main, prev, main, nxt, prev, main, nxt,
                  pl.BlockSpec((1, 2 * QBLK, KBLK), lambda bi, h, c: (h, 0, 0))],
        out_specs=[main, main],
        out_shape=[jax.ShapeDtypeStruct((b, hp, length, width), BF16),
                   jax.ShapeDtypeStruct((b, hp, length, width), F32)],
        compiler_params=_params(("parallel", "parallel", "parallel")),
        name=f"dilated_d{dil}",
    )(qv, kv, kv, kv, vv, vv, vv, bias)
    return o.reshape(b, hp, s, LANES), lse.reshape(b, hp, s, LANES)


def _na_kernel(q_ref, k_ref, v_ref, bias_ref, o_ref, *, rb, nrows):
    i = pl.program_id(2)
    lo = _lo_lanes()
    kspan = NA_ROWS * GRID_W
    for j in range(rb):
        r = i * rb + j
        rs = jnp.clip(r - NA_ROWS // 2, 0, nrows - NA_ROWS)
        start = pl.multiple_of(rs * GRID_W, GRID_W)
        kk = k_ref[0, 0, pl.ds(start, kspan), :]
        vv = v_ref[0, 0, pl.ds(start, kspan), :]
        qs = _stack_heads(q_ref[0, 0, j * GRID_W:(j + 1) * GRID_W, :], lo)
        s = lax.dot_general(qs, kk, _NT, preferred_element_type=F32) + bias_ref[0, r - rs]
        mx = jnp.max(s, axis=-1, keepdims=True)
        p = jnp.exp(s - mx)
        den = jnp.sum(p, axis=-1, keepdims=True)
        o = jnp.dot(p.astype(BF16), vv, preferred_element_type=F32) / den
        o_ref[0, 0, j * GRID_W:(j + 1) * GRID_W, :] = jnp.where(lo, o[:GRID_W], o[GRID_W:]).astype(BF16)


def _na_bias(rpb):
    t = np.arange(NA_ROWS)[:, None, None, None]
    krow = np.arange(NA_ROWS)[None, None, :, None]
    qc = np.arange(GRID_W)[None, :, None, None]
    kc = np.arange(GRID_W)[None, None, None, :]
    dr = np.broadcast_to(krow - t + NA_ROWS - 1, (NA_ROWS, GRID_W, NA_ROWS, GRID_W))
    dc = np.broadcast_to(np.clip(kc - qc + NA_COLS - 1, 0, 2 * NA_COLS - 2), dr.shape)
    cs = np.clip(qc - NA_COLS // 2, 0, GRID_W - NA_COLS)
    valid = np.broadcast_to((kc >= cs) & (kc < cs + NA_COLS), dr.shape)
    tab = rpb.astype(F32)[:, dr, dc]
    tab = jnp.where(valid[None], tab, NEG_INF).reshape(B_HEADS // 2, 2, NA_ROWS, GRID_W, NA_ROWS * GRID_W)
    return tab.transpose(0, 2, 1, 3, 4).reshape(B_HEADS // 2, NA_ROWS, 2 * GRID_W, NA_ROWS * GRID_W)


def _natten(q, k, v, bias, rb=8):
    b, hp, s, _ = q.shape
    nrows = s // GRID_W
    assert nrows >= NA_ROWS and nrows % rb == 0
    full = pl.BlockSpec((1, 1, s, LANES), lambda bi, h, i: (bi, h, 0, 0))
    tile = pl.BlockSpec((1, 1, rb * GRID_W, LANES), lambda bi, h, i: (bi, h, i, 0))
    return pl.pallas_call(
        functools.partial(_na_kernel, rb=rb, nrows=nrows),
        grid=(b, hp, nrows // rb),
        in_specs=[tile, full, full,
                  pl.BlockSpec((1,) + bias.shape[1:], lambda bi, h, i: (h, 0, 0, 0))],
        out_specs=tile,
        out_shape=jax.ShapeDtypeStruct(q.shape, BF16),
        compiler_params=_params(("parallel", "parallel", "arbitrary")),
        name="natten",
    )(q, k, v, bias)


def _s5_kernel(uf_ref, ub_ref, bf_ref, bb_ref, cf_ref, cb_ref, a_ref, yf_ref, yb_ref,
               xf_ref, xb_ref, st_ref, *, steps, nstate):
    @pl.when(pl.program_id(0) == 0)
    def _():
        st_ref[...] = jnp.zeros_like(st_ref)

    xf_ref[...] = jnp.dot(uf_ref[...].astype(BF16), bf_ref[...], preferred_element_type=F32)
    xb_ref[...] = jnp.dot(ub_ref[...].astype(BF16), bb_ref[...], preferred_element_type=F32)
    re, im = slice(0, nstate), slice(nstate, 2 * nstate)

    def step(t, carry):
        xrf, xif, xrb, xib = carry
        rf = pl.ds(pl.multiple_of(t * SUBLANES, SUBLANES), SUBLANES)
        rb = pl.ds(pl.multiple_of((steps - 1 - t) * SUBLANES, SUBLANES), SUBLANES)
        nrf = a_ref[0] * xrf - a_ref[1] * xif + xf_ref[rf, re]
        nif = a_ref[0] * xif + a_ref[1] * xrf + xf_ref[rf, im]
        nrb = a_ref[2] * xrb - a_ref[3] * xib + xb_ref[rb, re]
        nib = a_ref[2] * xib + a_ref[3] * xrb + xb_ref[rb, im]
        xf_ref[rf, re] = nrf
        xf_ref[rf, im] = nif
        xb_ref[rb, re] = nrb
        xb_ref[rb, im] = nib
        return nrf, nif, nrb, nib

    carry = lax.fori_loop(0, steps, step, (st_ref[0], st_ref[1], st_ref[2], st_ref[3]))
    for n, val in enumerate(carry):
        st_ref[n] = val
    yf_ref[...] = jnp.dot(xf_ref[...].astype(BF16), cf_ref[...], preferred_element_type=F32)
    yb_ref[...] = jnp.dot(xb_ref[...].astype(BF16), cb_ref[...], preferred_element_type=F32)


def _s5_params(a_re, a_im, log_dt, b_re, b_im, c_re, c_im):
    groups = a_re.shape[0]
    dt = jnp.exp(log_dt.astype(F32))[:, None]
    a = jnp.minimum(a_re.astype(F32), -1e-4)
    w = a_im.astype(F32)
    mag = jnp.exp(a * dt)
    abar_r, abar_i = mag * jnp.cos(w * dt), mag * jnp.sin(w * dt)
    den = a * a + w * w
    zr = abar_r - 1.0
    gr = (zr * a + abar_i * w) / den
    gi = (abar_i * a - zr * w) / den
    br, bi = b_re.astype(F32), b_im.astype(F32)
    bbar_r = gr[..., None] * br - gi[..., None] * bi
    bbar_i = gr[..., None] * bi + gi[..., None] * br
    eye = jnp.eye(groups, dtype=F32)
    n = groups * S5_STATE

    def in_mat(t):
        return jnp.einsum('gpc,gh->gchp', t, eye).reshape(groups * S5_GROUP, n)

    def out_mat(t):
        return jnp.einsum('gcp,gh->gphc', t, eye).reshape(n, groups * S5_GROUP)

    bmat = jnp.concatenate([in_mat(bbar_r), in_mat(bbar_i)], axis=1).astype(BF16)
    cmat = jnp.concatenate([out_mat(c_re.astype(F32)), out_mat(-c_im.astype(F32))], axis=0).astype(BF16)
    return abar_r.reshape(n), abar_i.reshape(n), bmat, cmat


def _s5(u2, bmat_f, bmat_b, cmat_f, cmat_b, avec, steps):
    rows, cw = u2.shape
    nstate = avec.shape[-1]
    tr = steps * SUBLANES
    n = rows // tr
    const = lambda shape: pl.BlockSpec(shape, lambda i: (0,) * len(shape))
    return pl.pallas_call(
        functools.partial(_s5_kernel, steps=steps, nstate=nstate),
        grid=(n,),
        in_specs=[pl.BlockSpec((tr, cw), lambda i: (i, 0)),
                  pl.BlockSpec((tr, cw), lambda i: (n - 1 - i, 0)),
                  const(bmat_f.shape), const(bmat_b.shape), const(cmat_f.shape), const(cmat_b.shape),
                  const(avec.shape)],
        out_specs=[pl.BlockSpec((tr, cw), lambda i: (i, 0)),
                   pl.BlockSpec((tr, cw), lambda i: (n - 1 - i, 0))],
        out_shape=[jax.ShapeDtypeStruct(u2.shape, F32), jax.ShapeDtypeStruct(u2.shape, F32)],
        scratch_shapes=[pltpu.VMEM((tr, 2 * nstate), F32), pltpu.VMEM((tr, 2 * nstate), F32),
                        pltpu.VMEM((4, SUBLANES, nstate), F32)],
        compiler_params=_params(("arbitrary",)),
        name="s5_scan",
    )(u2, u2, bmat_f, bmat_b, cmat_f, cmat_b, avec)


def _rms(t, gain):
    return (t * lax.rsqrt(jnp.mean(t * t, axis=-1, keepdims=True) + EPS)) * gain


def _out_proj_kernel(o1_ref, o2_ref, o3_ref, l1_ref, l2_ref, l3_ref, ob_ref, yf_ref, yb_ref, u_ref, x_ref,
                     ga_ref, gb_ref, gc_ref, dsk_ref, wglu_ref, wout_ref, gf_ref, wrh_ref, wrl_ref,
                     x1_ref, h_ref, aff_ref, *, a_pairs, b_pairs):
    parts = []
    for p in range(a_pairs):
        l1, l2, l3 = l1_ref[0, p], l2_ref[0, p], l3_ref[0, p]
        mx = jnp.maximum(jnp.maximum(l1, l2), l3)
        e1, e2, e3 = jnp.exp(l1 - mx), jnp.exp(l2 - mx), jnp.exp(l3 - mx)
        tot = e1 + e2 + e3
        parts.append((e1 / tot) * o1_ref[0, p].astype(F32) + (e2 / tot) * o2_ref[0, p].astype(F32)
                     + (e3 / tot) * o3_ref[0, p].astype(F32))
    oa = jnp.concatenate(parts, axis=-1)
    ob = jnp.concatenate([ob_ref[0, p].astype(F32) for p in range(b_pairs)], axis=-1)
    u = u_ref[...]
    y = (yf_ref[...] + yb_ref[...]) + dsk_ref[...] * u
    g = jax.nn.gelu(y)
    oc = g * jax.nn.sigmoid(jnp.dot(g.astype(BF16), wglu_ref[...], preferred_element_type=F32))
    mix = jnp.concatenate([_rms(oa, ga_ref[...]), _rms(ob, gb_ref[...]), _rms(oc, gc_ref[...])], axis=-1)
    x1 = x_ref[0] + jnp.dot(mix.astype(BF16), wout_ref[...], preferred_element_type=F32)
    x1_ref[0] = x1
    h = _rms(x1, gf_ref[...])
    h_hi = h.astype(BF16)
    h_lo = (h - h_hi.astype(F32)).astype(BF16)
    h_ref[0] = h_hi
    logits = (lax.dot_general(wrh_ref[...], h_hi, _NT, preferred_element_type=F32)
              + lax.dot_general(wrh_ref[...], h_lo, _NT, preferred_element_type=F32)
              + lax.dot_general(wrl_ref[...], h_hi, _NT, preferred_element_type=F32))
    e = jnp.exp(logits - jnp.max(logits, axis=0, keepdims=True))
    aff_ref[0] = e / jnp.sum(e, axis=0, keepdims=True)


def _out_proj(o_br, lse_br, ob, yf, yb, u2, x, ga, gb, gc, dsk, wglu, wout, gf, wrh, wrl, tm):
    b, s, d = x.shape
    a_pairs, b_pairs = A_HEADS // 2, B_HEADS // 2
    cw = u2.shape[1] // b
    ne = wrh.shape[0]
    spec_a = pl.BlockSpec((1, a_pairs, tm, LANES), lambda bi, i: (bi, 0, i, 0))
    spec_b = pl.BlockSpec((1, b_pairs, tm, LANES), lambda bi, i: (bi, 0, i, 0))
    spec_c = pl.BlockSpec((tm, cw), lambda bi, i: (i, bi))
    spec_x = pl.BlockSpec((1, tm, d), lambda bi, i: (bi, i, 0))
    const = lambda a: pl.BlockSpec(a.shape, lambda bi, i: (0,) * a.ndim)
    consts = (ga, gb, gc, dsk, wglu, wout, gf, wrh, wrl)
    return pl.pallas_call(
        functools.partial(_out_proj_kernel, a_pairs=a_pairs, b_pairs=b_pairs),
        grid=(b, s // tm),
        in_specs=[spec_a] * 6 + [spec_b, spec_c, spec_c, spec_c, spec_x] + [const(a) for a in consts],
        out_specs=[spec_x, spec_x, pl.BlockSpec((1, ne, tm), lambda bi, i: (bi, 0, i))],
        out_shape=[jax.ShapeDtypeStruct((b, s, d), F32), jax.ShapeDtypeStruct((b, s, d), BF16),
                   jax.ShapeDtypeStruct((b, ne, s), F32)],
        compiler_params=_params(("parallel", "parallel")),
        name="out_proj",
    )(*o_br, *lse_br, ob, yf, yb, u2, x, *consts)


def _moe_ffn_kernel(xe_ref, gate_ref, wg_ref, wu_ref, wd_ref, ye_ref, *, fchunk):
    xe = xe_ref[0, 0]
    ff = wg_ref.shape[2]
    acc = None
    for f in range(0, ff, fchunk):
        g = jnp.dot(xe, wg_ref[0, :, f:f + fchunk], preferred_element_type=F32)
        up = jnp.dot(xe, wu_ref[0, :, f:f + fchunk], preferred_element_type=F32)
        hid = (jax.nn.silu(g) * up).astype(BF16)
        part = jnp.dot(hid, wd_ref[0, f:f + fchunk, :], preferred_element_type=F32)
        acc = part if acc is None else acc + part
    ye_ref[0, 0] = acc * gate_ref[0, 0]


def _moe_ffn(xe, gate, wg, wu, wd, fchunk=512):
    b, ne, cap, d = xe.shape
    ff = wg.shape[2]
    tok = pl.BlockSpec((1, 1, cap, d), lambda e, bi: (bi, e, 0, 0))
    return pl.pallas_call(
        functools.partial(_moe_ffn_kernel, fchunk=min(fchunk, ff)),
        grid=(ne, b),
        in_specs=[tok, pl.BlockSpec((1, 1, cap, 1), lambda e, bi: (bi, e, 0, 0)),
                  pl.BlockSpec((1, d, ff), lambda e, bi: (e, 0, 0)),
                  pl.BlockSpec((1, d, ff), lambda e, bi: (e, 0, 0)),
                  pl.BlockSpec((1, ff, d), lambda e, bi: (e, 0, 0))],
        out_specs=tok,
        out_shape=jax.ShapeDtypeStruct((b, ne, cap, d), F32),
        compiler_params=_params(("parallel", "arbitrary")),
        name="moe_ffn",
    )(xe, gate, wg, wu, wd)


def _tile2(g):
    return jnp.concatenate([g, g], axis=-1)


def kernel(x, attn_norm, w_in, q_norm_a, k_norm_a, q_norm_b, k_norm_b, rel_pos_bias, s5_a_re, s5_a_im, s5_log_dt, s5_b_re, s5_b_im, s5_c_re, s5_c_im, s5_d, w_glu, out_norm_a, out_norm_b, out_norm_c, w_out, ffn_norm, w_router, w_gate, w_up, w_down):
    b, s, d = x.shape
    depth = w_in.shape[0]
    assert b == SUBLANES, "the S5 scan keeps one sequence per sublane"
    tm = min(512, s)
    steps = min(64, s)
    cap = EC_CAPACITY * s // N_EXPERTS
    scale = HEAD_DIM ** -0.5

    w_in_b, w_out_b, w_glu_b = w_in.astype(BF16), w_out.astype(BF16), w_glu.astype(BF16)
    w_gate_b, w_up_b, w_down_b = w_gate.astype(BF16), w_up.astype(BF16), w_down.astype(BF16)
    wr_t = jnp.swapaxes(w_router, 1, 2)
    wr_hi = wr_t.astype(BF16)
    wr_lo = (wr_t - wr_hi.astype(F32)).astype(BF16)

    for l in range(depth):
        hg = jnp.stack([_tile2(q_norm_a[l]) * scale, _tile2(k_norm_a[l]),
                        _tile2(q_norm_b[l]) * scale, _tile2(k_norm_b[l])], axis=0)
        qa, ka, va, qb, kb, vb, u2 = _in_proj(x, attn_norm[l][None], w_in_b[l], hg, tm)

        branches = [_dilated_branch(qa, ka, va, dil) for _, dil in DILATED_CONFIGS]
        ob = _natten(qb, kb, vb, _na_bias(rel_pos_bias[l]))

        pf = _s5_params(s5_a_re[l, 0], s5_a_im[l, 0], s5_log_dt[l, 0], s5_b_re[l, 0], s5_b_im[l, 0],
                        s5_c_re[l, 0], s5_c_im[l, 0])
        pb = _s5_params(s5_a_re[l, 1], s5_a_im[l, 1], s5_log_dt[l, 1], s5_b_re[l, 1], s5_b_im[l, 1],
                        s5_c_re[l, 1], s5_c_im[l, 1])
        avec = jnp.broadcast_to(jnp.stack([pf[0], pf[1], pb[0], pb[1]])[:, None, :],
                                (4, SUBLANES, pf[0].shape[0]))
        cw = u2.shape[1] // b
        yf, yb = _s5(u2.reshape(s * b, cw), pf[2], pb[2], pf[3], pb[3], avec, steps)

        x1, h, aff = _out_proj([o for o, _ in branches], [ls for _, ls in branches], ob,
                               yf.reshape(s, b * cw), yb.reshape(s, b * cw), u2, x,
                               out_norm_a[l][None], out_norm_b[l][None], out_norm_c[l][None], s5_d[l][None],
                               w_glu_b[l], w_out_b[l], ffn_norm[l][None], wr_hi[l], wr_lo[l], tm)

        gate, idx = lax.top_k(aff, cap)
        xe = jnp.take_along_axis(h[:, None], idx[..., None], axis=2)
        ye = _moe_ffn(xe, gate[..., None], w_gate_b[l], w_up_b[l], w_down_b[l])
        x = jax.vmap(lambda xb, ib, yb_: xb.at[ib.reshape(-1)].add(yb_.reshape(-1, d)))(x1, idx, ye)
    return x
```

```python
import functools

import numpy as np
import jax
import jax.numpy as jnp
from jax import lax
from jax.experimental import pallas as pl
from jax.experimental.pallas import tpu as pltpu

HEAD_DIM = 64
A_HEADS = 8
B_HEADS = 4
DILATED_CONFIGS = ((128, 1), (512, 4), (2048, 16))
GRID_W = 64
NA_ROWS = 8
NA_COLS = 16
S5_GROUP = 16
S5_STATE = 64
N_EXPERTS = 16
EC_CAPACITY = 2
EPS = 1e-6
NEG_INF = -1e30

LANES = 128
SUBLANES = 8
VMEM_LIMIT = 56 * 1024 * 1024

HALF = 64
QBLK = 2 * HALF
KBLK = 4 * HALF
DIL_CHUNK = 2048
DILS = tuple(d for _, d in DILATED_CONFIGS)
assert all(w == 2 * HALF * d for w, d in DILATED_CONFIGS) and DILS[0] == 1

F32 = jnp.float32
BF16 = jnp.bfloat16

_NT = (((1,), (1,)), ((), ()))


def _params(sem, vmem=VMEM_LIMIT, **kw):
    return pltpu.CompilerParams(dimension_semantics=sem, vmem_limit_bytes=vmem, **kw)


def _lo_lanes():
    return lax.broadcasted_iota(jnp.int32, (1, LANES), 1) < HEAD_DIM


def _stack_heads(q2, lo):
    zero = jnp.zeros_like(q2)
    return jnp.concatenate([jnp.where(lo, q2, zero), jnp.where(lo, zero, q2)], axis=0)


def _rms(t, gain):
    return (t * lax.rsqrt(jnp.mean(t * t, axis=-1, keepdims=True) + EPS)) * gain


def _in_proj_kernel(x_ref, g_ref, w_ref, hg_ref, *rest, a_pairs, b_pairs, tm):
    nd = len(DILS)
    qa_refs, ka_refs, va_refs = rest[0:nd], rest[nd:2 * nd], rest[2 * nd:3 * nd]
    qb_ref, kb_ref, vb_ref, u_ref, scr = rest[3 * nd:]
    h = _rms(x_ref[...], g_ref[...])
    proj = jnp.dot(h.astype(BF16), w_ref[...], preferred_element_type=F32)
    lo = _lo_lanes()

    def head_norm(c, gain_row):
        if gain_row is None:
            return c
        sq = c * c
        s_lo = jnp.sum(jnp.where(lo, sq, 0.0), axis=-1, keepdims=True)
        s_hi = jnp.sum(jnp.where(lo, 0.0, sq), axis=-1, keepdims=True)
        r = jnp.where(lo, lax.rsqrt(s_lo / HEAD_DIM + EPS), lax.rsqrt(s_hi / HEAD_DIM + EPS))
        return (c * r) * hg_ref[gain_row:gain_row + 1, :]

    col = 0
    nscr = 0
    for refs, gain_row in ((qa_refs, 0), (ka_refs, 1), (va_refs, None)):
        for p in range(a_pairs):
            c = head_norm(proj[:, col:col + LANES], gain_row)
            refs[0][0, p] = c.astype(BF16)
            scr[nscr] = c
            for ref, dil in zip(refs[1:], DILS[1:]):
                for r in range(dil):
                    ref[0, p, :, r * LANES:(r + 1) * LANES] = scr[nscr, pl.ds(r, tm // dil, stride=dil), :].astype(BF16)
            nscr += 1
            col += LANES
    for ref, gain_row in ((qb_ref, 2), (kb_ref, 3), (vb_ref, None)):
        for p in range(b_pairs):
            ref[0, p] = head_norm(proj[:, col:col + LANES], gain_row).astype(BF16)
            col += LANES
    u_ref[...] = proj[:, col:]


def _in_proj(xz, colblk, b, s, g, w, hg, tm):
    d = g.shape[1]
    nt = s // tm
    a_pairs, b_pairs = A_HEADS // 2, B_HEADS // 2
    cw = w.shape[1] - 3 * (a_pairs + b_pairs) * LANES
    shapes_a = [jax.ShapeDtypeStruct((b, a_pairs, s // dil, dil * LANES), BF16) for dil in DILS]
    specs_a = [pl.BlockSpec((1, a_pairs, tm // dil, dil * LANES), lambda bi, i: (bi, 0, i, 0)) for dil in DILS]
    shape_b = jax.ShapeDtypeStruct((b, b_pairs, s, LANES), BF16)
    spec_b = pl.BlockSpec((1, b_pairs, tm, LANES), lambda bi, i: (bi, 0, i, 0))
    return pl.pallas_call(
        functools.partial(_in_proj_kernel, a_pairs=a_pairs, b_pairs=b_pairs, tm=tm),
        grid=(b, nt),
        in_specs=[pl.BlockSpec((tm, d), lambda bi, i: (bi * nt + i, colblk)),
                  pl.BlockSpec((1, d), lambda bi, i: (0, 0)),
                  pl.BlockSpec(w.shape, lambda bi, i: (0, 0)),
                  pl.BlockSpec(hg.shape, lambda bi, i: (0, 0))],
        out_specs=specs_a * 3 + [spec_b] * 3 + [pl.BlockSpec((tm, cw), lambda bi, i: (i, bi))],
        out_shape=shapes_a * 3 + [shape_b] * 3 + [jax.ShapeDtypeStruct((s, b * cw), F32)],
        scratch_shapes=[pltpu.VMEM((3 * a_pairs, tm, LANES), F32)],
        compiler_params=_params(("parallel", "parallel")),
        name="in_proj",
    )(xz, g, w, hg)


def _dilated_kernel(*refs, chunk):
    nd = len(DILS)
    o_ref, acc_ref, lse_ref = refs[8 * nd:]
    c = pl.program_id(2)
    nc = pl.num_programs(2)
    lo = _lo_lanes()
    kcol = lax.broadcasted_iota(jnp.int32, (1, KBLK), 1)
    edge_first = jnp.where((kcol < HALF) & (c == 0), NEG_INF, 0.0)
    edge_last = jnp.where((kcol >= KBLK - HALF) & (c == nc - 1), NEG_INF, 0.0)
    for bi, dil in enumerate(DILS):
        q_ref, kp_ref, k_ref, kn_ref, vp_ref, v_ref, vn_ref, bias_ref = refs[8 * bi:8 * bi + 8]
        nblk = chunk // dil // QBLK
        bias = bias_ref[0]
        for r in range(dil):
            cs = slice(r * LANES, (r + 1) * LANES)
            kc = jnp.concatenate([kp_ref[0, 0, :, cs], k_ref[0, 0, :, cs], kn_ref[0, 0, :, cs]], axis=0)
            vc = jnp.concatenate([vp_ref[0, 0, :, cs], v_ref[0, 0, :, cs], vn_ref[0, 0, :, cs]], axis=0)
            for m in range(nblk):
                qs = _stack_heads(q_ref[0, 0, m * QBLK:(m + 1) * QBLK, cs], lo)
                kk = kc[m * QBLK:m * QBLK + KBLK]
                vv = vc[m * QBLK:m * QBLK + KBLK]
                s = lax.dot_general(qs, kk, _NT, preferred_element_type=F32) + bias
                if m == 0:
                    s = s + edge_first
                if m == nblk - 1:
                    s = s + edge_last
                mx = jnp.max(s, axis=-1, keepdims=True)
                p = jnp.exp(s - mx)
                den = jnp.sum(p, axis=-1, keepdims=True)
                o = jnp.dot(p.astype(BF16), vv, preferred_element_type=F32) / den
                lse = mx + jnp.log(den)
                o2 = jnp.where(lo, o[:QBLK], o[QBLK:])
                l2 = jnp.where(lo, lse[:QBLK], lse[QBLK:])
                tok = pl.ds(m * QBLK * dil + r, QBLK, stride=dil) if dil > 1 else pl.ds(m * QBLK, QBLK)
                if bi == 0:
                    acc_ref[tok, :] = o2
                    lse_ref[tok, :] = l2
                else:
                    a_old, l_old = acc_ref[tok, :], lse_ref[tok, :]
                    m2 = jnp.maximum(l_old, l2)
                    e_old, e_new = jnp.exp(l_old - m2), jnp.exp(l2 - m2)
                    tot = e_old + e_new
                    acc_ref[tok, :] = (a_old * e_old + o2 * e_new) / tot
                    if bi < nd - 1:
                        lse_ref[tok, :] = m2 + jnp.log(tot)
    o_ref[0, 0] = acc_ref[...].astype(BF16)


def _alibi_slopes(n):
    return np.array([2.0 ** (-8.0 * (i + 1) / n) for i in range(n)], dtype=np.float32)


def _dilated_bias(dil):
    rel = (np.arange(KBLK)[None, :] - HALF) - np.arange(QBLK)[:, None]
    valid = np.abs(rel) <= HALF
    dist = (dil * np.abs(rel)).astype(np.float32)
    slopes = _alibi_slopes(A_HEADS)
    per_head = np.where(valid[None], -(slopes[:, None, None] * dist[None]), np.float32(NEG_INF)).astype(np.float32)
    return per_head.reshape(A_HEADS // 2, 2 * QBLK, KBLK)


def _dilated(q_views, k_views, v_views):
    b, hp, s, _ = q_views[0].shape
    chunk = min(DIL_CHUNK, s)
    assert s % chunk == 0 and chunk % (QBLK * DILS[-1]) == 0
    args, specs = [], []
    for q, k, v, dil in zip(q_views, k_views, v_views, DILS):
        rows, width = chunk // dil, dil * LANES
        hb = rows // HALF
        last = s // dil // HALF - 1
        main = pl.BlockSpec((1, 1, rows, width), lambda bi, h, c: (bi, h, c, 0))
        prev = pl.BlockSpec((1, 1, HALF, width), lambda bi, h, c, hb=hb: (bi, h, jnp.maximum(c * hb - 1, 0), 0))
        nxt = pl.BlockSpec((1, 1, HALF, width),
                           lambda bi, h, c, hb=hb, last=last: (bi, h, jnp.minimum((c + 1) * hb, last), 0))
        args += [q, k, k, k, v, v, v, jnp.asarray(_dilated_bias(dil))]
        specs += [main, prev, main, nxt, prev, main, nxt,
                  pl.BlockSpec((1, 2 * QBLK, KBLK), lambda bi, h, c: (h, 0, 0))]
    return pl.pallas_call(
        functools.partial(_dilated_kernel, chunk=chunk),
        grid=(b, hp, s // chunk),
        in_specs=specs,
        out_specs=pl.BlockSpec((1, 1, chunk, LANES), lambda bi, h, c: (bi, h, c, 0)),
        out_shape=jax.ShapeDtypeStruct((b, hp, s, LANES), BF16),
        scratch_shapes=[pltpu.VMEM((chunk, LANES), F32), pltpu.VMEM((chunk, LANES), F32)],
        compiler_params=_params(("parallel", "parallel", "parallel")),
        name="dilated",
    )(*args)


def _na_kernel(q_ref, k_ref, v_ref, bias_ref, o_ref, *, rb, nrows):
    i = pl.program_id(2)
    lo = _lo_lanes()
    kspan = NA_ROWS * GRID_W
    for j in range(rb):
        r = i * rb + j
        rs = jnp.clip(r - NA_ROWS // 2, 0, nrows - NA_ROWS)
        start = pl.multiple_of(rs * GRID_W, GRID_W)
        kk = k_ref[0, 0, pl.ds(start, kspan), :]
        vv = v_ref[0, 0, pl.ds(start, kspan), :]
        qs = _stack_heads(q_ref[0, 0, j * GRID_W:(j + 1) * GRID_W, :], lo)
        s = lax.dot_general(qs, kk, _NT, preferred_element_type=F32) + bias_ref[0, r - rs]
        mx = jnp.max(s, axis=-1, keepdims=True)
        p = jnp.exp(s - mx)
        den = jnp.sum(p, axis=-1, keepdims=True)
        o = jnp.dot(p.astype(BF16), vv, preferred_element_type=F32) / den
        o_ref[0, 0, j * GRID_W:(j + 1) * GRID_W, :] = jnp.where(lo, o[:GRID_W], o[GRID_W:]).astype(BF16)


def _na_bias(rpb):
    t = np.arange(NA_ROWS, dtype=np.int32)[:, None, None, None]
    krow = np.arange(NA_ROWS, dtype=np.int32)[None, None, :, None]
    qc = np.arange(GRID_W, dtype=np.int32)[None, :, None, None]
    kc = np.arange(GRID_W, dtype=np.int32)[None, None, None, :]
    dr = np.broadcast_to(krow - t + NA_ROWS - 1, (NA_ROWS, GRID_W, NA_ROWS, GRID_W))
    dc = np.broadcast_to(np.clip(kc - qc + NA_COLS - 1, 0, 2 * NA_COLS - 2), dr.shape)
    cs = np.clip(qc - NA_COLS // 2, 0, GRID_W - NA_COLS)
    valid = np.broadcast_to((kc >= cs) & (kc < cs + NA_COLS), dr.shape)
    tab = rpb.astype(F32)[:, dr, dc]
    tab = jnp.where(valid[None], tab, NEG_INF).reshape(B_HEADS // 2, 2, NA_ROWS, GRID_W, NA_ROWS * GRID_W)
    return tab.transpose(0, 2, 1, 3, 4).reshape(B_HEADS // 2, NA_ROWS, 2 * GRID_W, NA_ROWS * GRID_W)


def _natten(q, k, v, bias, rb=8):
    b, hp, s, _ = q.shape
    nrows = s // GRID_W
    assert nrows >= NA_ROWS and nrows % rb == 0
    full = pl.BlockSpec((1, 1, s, LANES), lambda bi, h, i: (bi, h, 0, 0))
    tile = pl.BlockSpec((1, 1, rb * GRID_W, LANES), lambda bi, h, i: (bi, h, i, 0))
    return pl.pallas_call(
        functools.partial(_na_kernel, rb=rb, nrows=nrows),
        grid=(b, hp, nrows // rb),
        in_specs=[tile, full, full,
                  pl.BlockSpec((1,) + bias.shape[1:], lambda bi, h, i: (h, 0, 0, 0))],
        out_specs=tile,
        out_shape=jax.ShapeDtypeStruct(q.shape, BF16),
        compiler_params=_params(("parallel", "parallel", "arbitrary")),
        name="natten",
    )(q, k, v, bias)


def _s5_kernel(uf_ref, ub_ref, bf_ref, bb_ref, cf_ref, cb_ref, a_ref, yf_ref, yb_ref,
               xf_ref, xb_ref, io_ref, st_ref, *, steps, nstate, cw):
    @pl.when(pl.program_id(0) == 0)
    def _():
        st_ref[...] = jnp.zeros_like(st_ref)

    nlt = cw // LANES

    def time_major(u_ref):
        for bi in range(SUBLANES):
            for j in range(nlt):
                col = bi * cw + j * LANES
                io_ref[j, pl.ds(bi, steps, stride=SUBLANES), :] = u_ref[:, col:col + LANES]
        return jnp.concatenate([io_ref[j] for j in range(nlt)], axis=-1).astype(BF16)

    xf_ref[...] = jnp.dot(time_major(uf_ref), bf_ref[...], preferred_element_type=F32)
    xb_ref[...] = jnp.dot(time_major(ub_ref), bb_ref[...], preferred_element_type=F32)
    re, im = slice(0, nstate), slice(nstate, 2 * nstate)

    def step(t, carry):
        xrf, xif, xrb, xib = carry
        rf = pl.ds(pl.multiple_of(t * SUBLANES, SUBLANES), SUBLANES)
        rb = pl.ds(pl.multiple_of((steps - 1 - t) * SUBLANES, SUBLANES), SUBLANES)
        nrf = a_ref[0] * xrf - a_ref[1] * xif + xf_ref[rf, re]
        nif = a_ref[0] * xif + a_ref[1] * xrf + xf_ref[rf, im]
        nrb = a_ref[2] * xrb - a_ref[3] * xib + xb_ref[rb, re]
        nib = a_ref[2] * xib + a_ref[3] * xrb + xb_ref[rb, im]
        xf_ref[rf, re] = nrf
        xf_ref[rf, im] = nif
        xb_ref[rb, re] = nrb
        xb_ref[rb, im] = nib
        return nrf, nif, nrb, nib

    carry = lax.fori_loop(0, steps, step, (st_ref[0], st_ref[1], st_ref[2], st_ref[3]))
    for n, val in enumerate(carry):
        st_ref[n] = val

    def batch_major(x_ref, c_ref, y_ref):
        y = jnp.dot(x_ref[...].astype(BF16), c_ref[...], preferred_element_type=F32)
        for j in range(nlt):
            io_ref[j] = y[:, j * LANES:(j + 1) * LANES]
        for bi in range(SUBLANES):
            for j in range(nlt):
                col = bi * cw + j * LANES
                y_ref[:, col:col + LANES] = io_ref[j, pl.ds(bi, steps, stride=SUBLANES), :]

    batch_major(xf_ref, cf_ref, yf_ref)
    batch_major(xb_ref, cb_ref, yb_ref)


def _s5_params(a_re, a_im, log_dt, b_re, b_im, c_re, c_im):
    groups = a_re.shape[0]
    dt = jnp.exp(log_dt.astype(F32))[:, None]
    a = jnp.minimum(a_re.astype(F32), -1e-4)
    w = a_im.astype(F32)
    mag = jnp.exp(a * dt)
    abar_r, abar_i = mag * jnp.cos(w * dt), mag * jnp.sin(w * dt)
    den = a * a + w * w
    zr = abar_r - 1.0
    gr = (zr * a + abar_i * w) / den
    gi = (abar_i * a - zr * w) / den
    br, bi = b_re.astype(F32), b_im.astype(F32)
    bbar_r = gr[..., None] * br - gi[..., None] * bi
    bbar_i = gr[..., None] * bi + gi[..., None] * br
    eye = jnp.eye(groups, dtype=F32)
    n = groups * S5_STATE

    def in_mat(t):
        return jnp.einsum('gpc,gh->gchp', t, eye).reshape(groups * S5_GROUP, n)

    def out_mat(t):
        return jnp.einsum('gcp,gh->gphc', t, eye).reshape(n, groups * S5_GROUP)

    bmat = jnp.concatenate([in_mat(bbar_r), in_mat(bbar_i)], axis=1).astype(BF16)
    cmat = jnp.concatenate([out_mat(c_re.astype(F32)), out_mat(-c_im.astype(F32))], axis=0).astype(BF16)
    return abar_r.reshape(n), abar_i.reshape(n), bmat, cmat


def _s5(u2, bmat_f, bmat_b, cmat_f, cmat_b, avec, steps):
    s, width = u2.shape
    cw = width // SUBLANES
    nstate = avec.shape[-1]
    n = s // steps
    tr = steps * SUBLANES
    const = lambda shape: pl.BlockSpec(shape, lambda i: (0,) * len(shape))
    return pl.pallas_call(
        functools.partial(_s5_kernel, steps=steps, nstate=nstate, cw=cw),
        grid=(n,),
        in_specs=[pl.BlockSpec((steps, width), lambda i: (i, 0)),
                  pl.BlockSpec((steps, width), lambda i: (n - 1 - i, 0)),
                  const(bmat_f.shape), const(bmat_b.shape), const(cmat_f.shape), const(cmat_b.shape),
                  const(avec.shape)],
        out_specs=[pl.BlockSpec((steps, width), lambda i: (i, 0)),
                   pl.BlockSpec((steps, width), lambda i: (n - 1 - i, 0))],
        out_shape=[jax.ShapeDtypeStruct(u2.shape, F32), jax.ShapeDtypeStruct(u2.shape, F32)],
        scratch_shapes=[pltpu.VMEM((tr, 2 * nstate), F32), pltpu.VMEM((tr, 2 * nstate), F32),
                        pltpu.VMEM((cw // LANES, tr, LANES), F32), pltpu.VMEM((4, SUBLANES, nstate), F32)],
        compiler_params=_params(("arbitrary",)),
        name="s5_scan",
    )(u2, u2, bmat_f, bmat_b, cmat_f, cmat_b, avec)


def _out_proj_kernel(oa_ref, ob_ref, yf_ref, yb_ref, u_ref, x_ref,
                     ga_ref, gb_ref, gc_ref, dsk_ref, wglu_ref, wout_ref, gf_ref, wrh_ref, wrl_ref,
                     z_ref, aff_ref, *, a_pairs, b_pairs):
    oa = jnp.concatenate([oa_ref[0, p].astype(F32) for p in range(a_pairs)], axis=-1)
    ob = jnp.concatenate([ob_ref[0, p].astype(F32) for p in range(b_pairs)], axis=-1)
    y = (yf_ref[...] + yb_ref[...]) + dsk_ref[...] * u_ref[...]
    g = jax.nn.gelu(y)
    oc = g * jax.nn.sigmoid(jnp.dot(g.astype(BF16), wglu_ref[...], preferred_element_type=F32))
    mix = jnp.concatenate([_rms(oa, ga_ref[...]), _rms(ob, gb_ref[...]), _rms(oc, gc_ref[...])], axis=-1)
    x1 = x_ref[...] + jnp.dot(mix.astype(BF16), wout_ref[...], preferred_element_type=F32)
    d = x1.shape[1]
    z_ref[:, 0:d] = x1
    z_ref[:, d:2 * d] = x1
    h = _rms(x1, gf_ref[...])
    h_hi = h.astype(BF16)
    h_lo = (h - h_hi.astype(F32)).astype(BF16)
    logits = (lax.dot_general(wrh_ref[...], h_hi, _NT, preferred_element_type=F32)
              + lax.dot_general(wrh_ref[...], h_lo, _NT, preferred_element_type=F32)
              + lax.dot_general(wrl_ref[...], h_hi, _NT, preferred_element_type=F32))
    e = jnp.exp(logits - jnp.max(logits, axis=0, keepdims=True))
    aff_ref[0] = e / jnp.sum(e, axis=0, keepdims=True)


def _out_proj(oa, ob, yf, yb, u2, xz, colblk, ga, gb, gc, dsk, wglu, wout, gf, wrh, wrl, tm):
    b, a_pairs, s, _ = oa.shape
    b_pairs = ob.shape[1]
    d = wout.shape[1]
    nt = s // tm
    cw = u2.shape[1] // b
    ne = wrh.shape[0]
    spec_a = pl.BlockSpec((1, a_pairs, tm, LANES), lambda bi, i: (bi, 0, i, 0))
    spec_b = pl.BlockSpec((1, b_pairs, tm, LANES), lambda bi, i: (bi, 0, i, 0))
    spec_c = pl.BlockSpec((tm, cw), lambda bi, i: (i, bi))
    const = lambda a: pl.BlockSpec(a.shape, lambda bi, i: (0,) * a.ndim)
    consts = (ga, gb, gc, dsk, wglu, wout, gf, wrh, wrl)
    return pl.pallas_call(
        functools.partial(_out_proj_kernel, a_pairs=a_pairs, b_pairs=b_pairs),
        grid=(b, nt),
        in_specs=[spec_a, spec_b, spec_c, spec_c, spec_c,
                  pl.BlockSpec((tm, d), lambda bi, i: (bi * nt + i, colblk))] + [const(a) for a in consts],
        out_specs=[pl.BlockSpec((tm, 2 * d), lambda bi, i: (bi * nt + i, 0)),
                   pl.BlockSpec((1, ne, tm), lambda bi, i: (bi, 0, i))],
        out_shape=[jax.ShapeDtypeStruct((b * s, 2 * d), F32), jax.ShapeDtypeStruct((b, ne, s), F32)],
        compiler_params=_params(("parallel", "parallel")),
        name="out_proj",
    )(oa, ob, yf, yb, u2, xz, *consts)


ROW_UNROLL = 8


def _moe_kernel(idx_ref, gate_ref, gf_ref, wg_ref, wu_ref, wd_ref, z_in_ref, z_ref,
                gbuf, obuf, gsem, ssem, *, ne, nb, cap, seq, d, fchunk):
    del z_in_ref
    e, b = pl.program_id(0), pl.program_id(1)
    n = e * nb + b
    slot = n % 2

    def row_copies(step, fn):
        e2, b2 = step // nb, step % nb
        base = (b2 * ne + e2) * cap
        tok0 = b2 * seq

        def body(i, carry):
            for k in range(ROW_UNROLL):
                r = i * ROW_UNROLL + k
                fn(r, idx_ref[base + r] + tok0)
            return carry

        lax.fori_loop(0, cap // ROW_UNROLL, body, 0)

    def gather(step, sl):
        row_copies(step, lambda r, tok: pltpu.make_async_copy(
            z_ref.at[pl.ds(tok, 1), :], gbuf.at[sl, pl.ds(r, 1), :], gsem.at[sl]).start())

    def scatter_copy(r, tok):
        return pltpu.make_async_copy(obuf.at[pl.ds(r, 1), :], z_ref.at[pl.ds(tok, 1), pl.ds(d, d)], ssem)

    @pl.when(n == 0)
    def _():
        gather(0, 0)

    pltpu.make_async_copy(z_ref.at[pl.ds(0, cap), :], gbuf.at[slot], gsem.at[slot]).wait()

    @pl.when(n + 1 < ne * nb)
    def _():
        gather(n + 1, 1 - slot)

    h = _rms(gbuf[slot, :, 0:d], gf_ref[...]).astype(BF16)
    ff = wg_ref.shape[2]
    y = None
    for f in range(0, ff, fchunk):
        g = jnp.dot(h, wg_ref[0, :, f:f + fchunk], preferred_element_type=F32)
        up = jnp.dot(h, wu_ref[0, :, f:f + fchunk], preferred_element_type=F32)
        hid = (jax.nn.silu(g) * up).astype(BF16)
        part = jnp.dot(hid, wd_ref[0, f:f + fchunk, :], preferred_element_type=F32)
        y = part if y is None else y + part

    @pl.when(n > 0)
    def _():
        pltpu.make_async_copy(obuf, z_ref.at[pl.ds(0, cap), pl.ds(d, d)], ssem).wait()

    obuf[...] = gbuf[slot, :, d:2 * d] + y * gate_ref[0, 0]
    row_copies(n, lambda r, tok: scatter_copy(r, tok).start())

    @pl.when(n == ne * nb - 1)
    def _():
        pltpu.make_async_copy(obuf, z_ref.at[pl.ds(0, cap), pl.ds(d, d)], ssem).wait()


def _moe(idx, gate, z, gf, wg, wu, wd, seq, fchunk=512):
    b, ne, cap = idx.shape
    d, ff = wg.shape[1], wg.shape[2]
    assert cap % ROW_UNROLL == 0
    once = pl.Buffered(1)
    grid_spec = pltpu.PrefetchScalarGridSpec(
        num_scalar_prefetch=1,
        grid=(ne, b),
        in_specs=[pl.BlockSpec((1, 1, cap, 1), lambda e, bi, idx: (bi, e, 0, 0)),
                  pl.BlockSpec((1, d), lambda e, bi, idx: (0, 0)),
                  pl.BlockSpec((1, d, ff), lambda e, bi, idx: (e, 0, 0), pipeline_mode=once),
                  pl.BlockSpec((1, d, ff), lambda e, bi, idx: (e, 0, 0), pipeline_mode=once),
                  pl.BlockSpec((1, ff, d), lambda e, bi, idx: (e, 0, 0), pipeline_mode=once),
                  pl.BlockSpec(memory_space=pl.ANY)],
        out_specs=pl.BlockSpec(memory_space=pl.ANY),
        scratch_shapes=[pltpu.VMEM((2, cap, 2 * d), F32), pltpu.VMEM((cap, d), F32),
                        pltpu.SemaphoreType.DMA((2,)), pltpu.SemaphoreType.DMA(())],
    )
    return pl.pallas_call(
        functools.partial(_moe_kernel, ne=ne, nb=b, cap=cap, seq=seq, d=d, fchunk=min(fchunk, ff)),
        grid_spec=grid_spec,
        out_shape=jax.ShapeDtypeStruct(z.shape, z.dtype),
        input_output_aliases={6: 0},
        compiler_params=_params(("arbitrary", "arbitrary"), disable_bounds_checks=True),
        name="moe",
    )(idx.reshape(-1), gate[..., None], gf, wg, wu, wd, z)


def _tile2(g):
    return jnp.concatenate([g, g], axis=-1)


def kernel(x, attn_norm, w_in, q_norm_a, k_norm_a, q_norm_b, k_norm_b, rel_pos_bias, s5_a_re, s5_a_im, s5_log_dt, s5_b_re, s5_b_im, s5_c_re, s5_c_im, s5_d, w_glu, out_norm_a, out_norm_b, out_norm_c, w_out, ffn_norm, w_router, w_gate, w_up, w_down):
    b, s, d = x.shape
    depth = w_in.shape[0]
    assert b == SUBLANES, "the S5 scan keeps one sequence per sublane"
    tm = min(512, s)
    steps = min(64, s)
    cap = EC_CAPACITY * s // N_EXPERTS
    scale = HEAD_DIM ** -0.5
    nd = len(DILS)

    w_in_b, w_out_b, w_glu_b = w_in.astype(BF16), w_out.astype(BF16), w_glu.astype(BF16)
    w_gate_b, w_up_b, w_down_b = w_gate.astype(BF16), w_up.astype(BF16), w_down.astype(BF16)
    wr_t = jnp.swapaxes(w_router, 1, 2)
    wr_hi = wr_t.astype(BF16)
    wr_lo = (wr_t - wr_hi.astype(F32)).astype(BF16)

    xz, colblk = x.reshape(b * s, d), 0
    for l in range(depth):
        hg = jnp.stack([_tile2(q_norm_a[l]) * scale, _tile2(k_norm_a[l]),
                        _tile2(q_norm_b[l]) * scale, _tile2(k_norm_b[l])], axis=0)
        outs = _in_proj(xz, colblk, b, s, attn_norm[l][None], w_in_b[l], hg, tm)
        qa, ka, va = outs[0:nd], outs[nd:2 * nd], outs[2 * nd:3 * nd]
        qb, kb, vb, u2 = outs[3 * nd:]

        oa = _dilated(qa, ka, va)
        ob = _natten(qb, kb, vb, _na_bias(rel_pos_bias[l]))

        pf = _s5_params(s5_a_re[l, 0], s5_a_im[l, 0], s5_log_dt[l, 0], s5_b_re[l, 0], s5_b_im[l, 0],
                        s5_c_re[l, 0], s5_c_im[l, 0])
        pb = _s5_params(s5_a_re[l, 1], s5_a_im[l, 1], s5_log_dt[l, 1], s5_b_re[l, 1], s5_b_im[l, 1],
                        s5_c_re[l, 1], s5_c_im[l, 1])
        avec = jnp.broadcast_to(jnp.stack([pf[0], pf[1], pb[0], pb[1]])[:, None, :],
                                (4, SUBLANES, pf[0].shape[0]))
        yf, yb = _s5(u2, pf[2], pb[2], pf[3], pb[3], avec, steps)

        z, aff = _out_proj(oa, ob, yf, yb, u2, xz, colblk,
                           out_norm_a[l][None], out_norm_b[l][None], out_norm_c[l][None], s5_d[l][None],
                           w_glu_b[l], w_out_b[l], ffn_norm[l][None], wr_hi[l], wr_lo[l], tm)

        gate, idx = lax.top_k(aff, cap)
        xz = _moe(idx, gate, z, ffn_norm[l][None], w_gate_b[l], w_up_b[l], w_down_b[l], s)
        colblk = 1
    return xz[:, d:].reshape(b, s, d)
```

```python
import functools

import numpy as np
import jax
import jax.numpy as jnp
from jax import lax
from jax.experimental import pallas as pl
from jax.experimental.pallas import tpu as pltpu

HEAD_DIM = 64
A_HEADS = 8
B_HEADS = 4
DILATED_CONFIGS = ((128, 1), (512, 4), (2048, 16))
GRID_W = 64
NA_ROWS = 8
NA_COLS = 16
S5_GROUP = 16
S5_STATE = 64
N_EXPERTS = 16
EC_CAPACITY = 2
EPS = 1e-6
NEG_INF = -1e30

LANES = 128
SUBLANES = 8
VMEM_LIMIT = 56 * 1024 * 1024

HALF = 64
QBLK = 2 * HALF
KBLK = 4 * HALF
DIL_CHUNK = 2048
DILS = tuple(d for _, d in DILATED_CONFIGS)
assert all(w == 2 * HALF * d for w, d in DILATED_CONFIGS) and DILS[0] == 1

F32 = jnp.float32
BF16 = jnp.bfloat16

_NT = (((1,), (1,)), ((), ()))


def _params(sem, vmem=VMEM_LIMIT, **kw):
    return pltpu.CompilerParams(dimension_semantics=sem, vmem_limit_bytes=vmem, **kw)


def _lo_lanes():
    return lax.broadcasted_iota(jnp.int32, (1, LANES), 1) < HEAD_DIM


def _stack_heads(q2, lo):
    zero = jnp.zeros_like(q2)
    return jnp.concatenate([jnp.where(lo, q2, zero), jnp.where(lo, zero, q2)], axis=0)


def _rms(t, gain):
    return (t * lax.rsqrt(jnp.mean(t * t, axis=-1, keepdims=True) + EPS)) * gain


def _in_proj_kernel(x_ref, g_ref, w_ref, hg_ref, *rest, a_pairs, b_pairs, tm):
    nd = len(DILS)
    qa_refs, ka_refs, va_refs = rest[0:nd], rest[nd:2 * nd], rest[2 * nd:3 * nd]
    qb_ref, kb_ref, vb_ref, u_ref, scr = rest[3 * nd:]
    h = _rms(x_ref[...], g_ref[...])
    proj = jnp.dot(h.astype(BF16), w_ref[...], preferred_element_type=F32)
    lo = _lo_lanes()

    def head_norm(c, gain_row):
        if gain_row is None:
            return c
        sq = c * c
        s_lo = jnp.sum(jnp.where(lo, sq, 0.0), axis=-1, keepdims=True)
        s_hi = jnp.sum(jnp.where(lo, 0.0, sq), axis=-1, keepdims=True)
        r = jnp.where(lo, lax.rsqrt(s_lo / HEAD_DIM + EPS), lax.rsqrt(s_hi / HEAD_DIM + EPS))
        return (c * r) * hg_ref[gain_row:gain_row + 1, :]

    col = 0
    nscr = 0
    for refs, gain_row in ((qa_refs, 0), (ka_refs, 1), (va_refs, None)):
        for p in range(a_pairs):
            c = head_norm(proj[:, col:col + LANES], gain_row)
            refs[0][0, p] = c.astype(BF16)
            scr[nscr] = c
            for ref, dil in zip(refs[1:], DILS[1:]):
                for r in range(dil):
                    ref[0, p, :, r * LANES:(r + 1) * LANES] = scr[nscr, pl.ds(r, tm // dil, stride=dil), :].astype(BF16)
            nscr += 1
            col += LANES
    for ref, gain_row in ((qb_ref, 2), (kb_ref, 3), (vb_ref, None)):
        for p in range(b_pairs):
            ref[0, p] = head_norm(proj[:, col:col + LANES], gain_row).astype(BF16)
            col += LANES
    u_ref[...] = proj[:, col:]


def _in_proj(xz, colblk, b, s, g, w, hg, tm):
    d = g.shape[1]
    nt = s // tm
    a_pairs, b_pairs = A_HEADS // 2, B_HEADS // 2
    cw = w.shape[1] - 3 * (a_pairs + b_pairs) * LANES
    shapes_a = [jax.ShapeDtypeStruct((b, a_pairs, s // dil, dil * LANES), BF16) for dil in DILS]
    specs_a = [pl.BlockSpec((1, a_pairs, tm // dil, dil * LANES), lambda bi, i: (bi, 0, i, 0)) for dil in DILS]
    shape_b = jax.ShapeDtypeStruct((b, b_pairs, s, LANES), BF16)
    spec_b = pl.BlockSpec((1, b_pairs, tm, LANES), lambda bi, i: (bi, 0, i, 0))
    return pl.pallas_call(
        functools.partial(_in_proj_kernel, a_pairs=a_pairs, b_pairs=b_pairs, tm=tm),
        grid=(b, nt),
        in_specs=[pl.BlockSpec((tm, d), lambda bi, i: (bi * nt + i, colblk)),
                  pl.BlockSpec((1, d), lambda bi, i: (0, 0)),
                  pl.BlockSpec(w.shape, lambda bi, i: (0, 0)),
                  pl.BlockSpec(hg.shape, lambda bi, i: (0, 0))],
        out_specs=specs_a * 3 + [spec_b] * 3 + [pl.BlockSpec((tm, cw), lambda bi, i: (i, bi))],
        out_shape=shapes_a * 3 + [shape_b] * 3 + [jax.ShapeDtypeStruct((s, b * cw), F32)],
        scratch_shapes=[pltpu.VMEM((3 * a_pairs, tm, LANES), F32)],
        compiler_params=_params(("parallel", "parallel")),
        name="in_proj",
    )(xz, g, w, hg)


def _dilated_kernel(*refs, chunk):
    nd = len(DILS)
    o_ref, acc_ref, lse_ref = refs[8 * nd:]
    c = pl.program_id(2)
    nc = pl.num_programs(2)
    lo = _lo_lanes()
    kcol = lax.broadcasted_iota(jnp.int32, (1, KBLK), 1)
    edge_first = jnp.where((kcol < HALF) & (c == 0), NEG_INF, 0.0)
    edge_last = jnp.where((kcol >= KBLK - HALF) & (c == nc - 1), NEG_INF, 0.0)
    for bi, dil in enumerate(DILS):
        q_ref, kp_ref, k_ref, kn_ref, vp_ref, v_ref, vn_ref, bias_ref = refs[8 * bi:8 * bi + 8]
        nblk = chunk // dil // QBLK
        bias = bias_ref[0]
        for r in range(dil):
            cs = slice(r * LANES, (r + 1) * LANES)
            kc = jnp.concatenate([kp_ref[0, 0, :, cs], k_ref[0, 0, :, cs], kn_ref[0, 0, :, cs]], axis=0)
            vc = jnp.concatenate([vp_ref[0, 0, :, cs], v_ref[0, 0, :, cs], vn_ref[0, 0, :, cs]], axis=0)
            for m in range(nblk):
                qs = _stack_heads(q_ref[0, 0, m * QBLK:(m + 1) * QBLK, cs], lo)
                kk = kc[m * QBLK:m * QBLK + KBLK]
                vv = vc[m * QBLK:m * QBLK + KBLK]
                s = lax.dot_general(qs, kk, _NT, preferred_element_type=F32) + bias
                if m == 0:
                    s = s + edge_first
                if m == nblk - 1:
                    s = s + edge_last
                mx = jnp.max(s, axis=-1, keepdims=True)
                p = jnp.exp(s - mx)
                den = jnp.sum(p, axis=-1, keepdims=True)
                o = jnp.dot(p.astype(BF16), vv, preferred_element_type=F32) / den
                lse = mx + jnp.log(den)
                o2 = jnp.where(lo, o[:QBLK], o[QBLK:])
                l2 = jnp.where(lo, lse[:QBLK], lse[QBLK:])
                tok = pl.ds(m * QBLK * dil + r, QBLK, stride=dil) if dil > 1 else pl.ds(m * QBLK, QBLK)
                if bi == 0:
                    acc_ref[tok, :] = o2
                    lse_ref[tok, :] = l2
                else:
                    a_old, l_old = acc_ref[tok, :], lse_ref[tok, :]
                    m2 = jnp.maximum(l_old, l2)
                    e_old, e_new = jnp.exp(l_old - m2), jnp.exp(l2 - m2)
                    tot = e_old + e_new
                    acc_ref[tok, :] = (a_old * e_old + o2 * e_new) / tot
                    if bi < nd - 1:
                        lse_ref[tok, :] = m2 + jnp.log(tot)
    o_ref[0, 0] = acc_ref[...].astype(BF16)


def _alibi_slopes(n):
    return np.array([2.0 ** (-8.0 * (i + 1) / n) for i in range(n)], dtype=np.float32)


def _dilated_bias(dil):
    rel = (np.arange(KBLK)[None, :] - HALF) - np.arange(QBLK)[:, None]
    valid = np.abs(rel) <= HALF
    dist = (dil * np.abs(rel)).astype(np.float32)
    slopes = _alibi_slopes(A_HEADS)
    per_head = np.where(valid[None], -(slopes[:, None, None] * dist[None]), np.float32(NEG_INF)).astype(np.float32)
    return per_head.reshape(A_HEADS // 2, 2 * QBLK, KBLK)


def _dilated(q_views, k_views, v_views):
    b, hp, s, _ = q_views[0].shape
    chunk = min(DIL_CHUNK, s)
    assert s % chunk == 0 and chunk % (QBLK * DILS[-1]) == 0
    args, specs = [], []
    for q, k, v, dil in zip(q_views, k_views, v_views, DILS):
        rows, width = chunk // dil, dil * LANES
        hb = rows // HALF
        last = s // dil // HALF - 1
        main = pl.BlockSpec((1, 1, rows, width), lambda bi, h, c: (bi, h, c, 0))
        prev = pl.BlockSpec((1, 1, HALF, width), lambda bi, h, c, hb=hb: (bi, h, jnp.maximum(c * hb - 1, 0), 0))
        nxt = pl.BlockSpec((1, 1, HALF, width),
                           lambda bi, h, c, hb=hb, last=last: (bi, h, jnp.minimum((c + 1) * hb, last), 0))
        args += [q, k, k, k, v, v, v, jnp.asarray(_dilated_bias(dil))]
        specs += [main, prev, main, nxt, prev, main, nxt,
                  pl.BlockSpec((1, 2 * QBLK, KBLK), lambda bi, h, c: (h, 0, 0))]
    return pl.pallas_call(
        functools.partial(_dilated_kernel, chunk=chunk),
        grid=(b, hp, s // chunk),
        in_specs=specs,
        out_specs=pl.BlockSpec((1, 1, chunk, LANES), lambda bi, h, c: (bi, h, c, 0)),
        out_shape=jax.ShapeDtypeStruct((b, hp, s, LANES), BF16),
        scratch_shapes=[pltpu.VMEM((chunk, LANES), F32), pltpu.VMEM((chunk, LANES), F32)],
        compiler_params=_params(("parallel", "parallel", "parallel")),
        name="dilated",
    )(*args)


def _na_kernel(q_ref, k_ref, v_ref, bias_ref, o_ref, *, rb, nrows):
    i = pl.program_id(2)
    lo = _lo_lanes()
    kspan = NA_ROWS * GRID_W
    for j in range(rb):
        r = i * rb + j
        rs = jnp.clip(r - NA_ROWS // 2, 0, nrows - NA_ROWS)
        start = pl.multiple_of(rs * GRID_W, GRID_W)
        kk = k_ref[0, 0, pl.ds(start, kspan), :]
        vv = v_ref[0, 0, pl.ds(start, kspan), :]
        qs = _stack_heads(q_ref[0, 0, j * GRID_W:(j + 1) * GRID_W, :], lo)
        s = lax.dot_general(qs, kk, _NT, preferred_element_type=F32) + bias_ref[0, r - rs]
        mx = jnp.max(s, axis=-1, keepdims=True)
        p = jnp.exp(s - mx)
        den = jnp.sum(p, axis=-1, keepdims=True)
        o = jnp.dot(p.astype(BF16), vv, preferred_element_type=F32) / den
        o_ref[0, 0, j * GRID_W:(j + 1) * GRID_W, :] = jnp.where(lo, o[:GRID_W], o[GRID_W:]).astype(BF16)


def _na_bias(rpb):
    t = np.arange(NA_ROWS, dtype=np.int32)[:, None, None, None]
    krow = np.arange(NA_ROWS, dtype=np.int32)[None, None, :, None]
    qc = np.arange(GRID_W, dtype=np.int32)[None, :, None, None]
    kc = np.arange(GRID_W, dtype=np.int32)[None, None, None, :]
    dr = np.broadcast_to(krow - t + NA_ROWS - 1, (NA_ROWS, GRID_W, NA_ROWS, GRID_W))
    dc = np.broadcast_to(np.clip(kc - qc + NA_COLS - 1, 0, 2 * NA_COLS - 2), dr.shape)
    cs = np.clip(qc - NA_COLS // 2, 0, GRID_W - NA_COLS)
    valid = np.broadcast_to((kc >= cs) & (kc < cs + NA_COLS), dr.shape)
    tab = rpb.astype(F32)[:, dr, dc]
    tab = jnp.where(valid[None], tab, NEG_INF).reshape(B_HEADS // 2, 2, NA_ROWS, GRID_W, NA_ROWS * GRID_W)
    return tab.transpose(0, 2, 1, 3, 4).reshape(B_HEADS // 2, NA_ROWS, 2 * GRID_W, NA_ROWS * GRID_W)


def _natten(q, k, v, bias, rb=8):
    b, hp, s, _ = q.shape
    nrows = s // GRID_W
    assert nrows >= NA_ROWS and nrows % rb == 0
    full = pl.BlockSpec((1, 1, s, LANES), lambda bi, h, i: (bi, h, 0, 0))
    tile = pl.BlockSpec((1, 1, rb * GRID_W, LANES), lambda bi, h, i: (bi, h, i, 0))
    return pl.pallas_call(
        functools.partial(_na_kernel, rb=rb, nrows=nrows),
        grid=(b, hp, nrows // rb),
        in_specs=[tile, full, full,
                  pl.BlockSpec((1,) + bias.shape[1:], lambda bi, h, i: (h, 0, 0, 0))],
        out_specs=tile,
        out_shape=jax.ShapeDtypeStruct(q.shape, BF16),
        compiler_params=_params(("parallel", "parallel", "arbitrary")),
        name="natten",
    )(q, k, v, bias)


def _s5_kernel(uf_ref, ub_ref, bf_ref, bb_ref, cf_ref, cb_ref, a_ref, yf_ref, yb_ref,
               xf_ref, xb_ref, io_ref, st_ref, *, steps, nstate, cw):
    @pl.when(pl.program_id(0) == 0)
    def _():
        st_ref[...] = jnp.zeros_like(st_ref)

    nlt = cw // LANES

    def time_major(u_ref):
        for bi in range(SUBLANES):
            for j in range(nlt):
                col = bi * cw + j * LANES
                io_ref[j, pl.ds(bi, steps, stride=SUBLANES), :] = u_ref[:, col:col + LANES]
        return jnp.concatenate([io_ref[j] for j in range(nlt)], axis=-1).astype(BF16)

    xf_ref[...] = jnp.dot(time_major(uf_ref), bf_ref[...], preferred_element_type=F32)
    xb_ref[...] = jnp.dot(time_major(ub_ref), bb_ref[...], preferred_element_type=F32)
    re, im = slice(0, nstate), slice(nstate, 2 * nstate)

    def step(t, carry):
        xrf, xif, xrb, xib = carry
        rf = pl.ds(pl.multiple_of(t * SUBLANES, SUBLANES), SUBLANES)
        rb = pl.ds(pl.multiple_of((steps - 1 - t) * SUBLANES, SUBLANES), SUBLANES)
        nrf = a_ref[0] * xrf - a_ref[1] * xif + xf_ref[rf, re]
        nif = a_ref[0] * xif + a_ref[1] * xrf + xf_ref[rf, im]
        nrb = a_ref[2] * xrb - a_ref[3] * xib + xb_ref[rb, re]
        nib = a_ref[2] * xib + a_ref[3] * xrb + xb_ref[rb, im]
        xf_ref[rf, re] = nrf
        xf_ref[rf, im] = nif
        xb_ref[rb, re] = nrb
        xb_ref[rb, im] = nib
        return nrf, nif, nrb, nib

    carry = lax.fori_loop(0, steps, step, (st_ref[0], st_ref[1], st_ref[2], st_ref[3]))
    for n, val in enumerate(carry):
        st_ref[n] = val

    def batch_major(x_ref, c_ref, y_ref):
        y = jnp.dot(x_ref[...].astype(BF16), c_ref[...], preferred_element_type=F32)
        for j in range(nlt):
            io_ref[j] = y[:, j * LANES:(j + 1) * LANES]
        for bi in range(SUBLANES):
            for j in range(nlt):
                col = bi * cw + j * LANES
                y_ref[:, col:col + LANES] = io_ref[j, pl.ds(bi, steps, stride=SUBLANES), :]

    batch_major(xf_ref, cf_ref, yf_ref)
    batch_major(xb_ref, cb_ref, yb_ref)


def _s5_params(a_re, a_im, log_dt, b_re, b_im, c_re, c_im):
    groups = a_re.shape[0]
    dt = jnp.exp(log_dt.astype(F32))[:, None]
    a = jnp.minimum(a_re.astype(F32), -1e-4)
    w = a_im.astype(F32)
    mag = jnp.exp(a * dt)
    abar_r, abar_i = mag * jnp.cos(w * dt), mag * jnp.sin(w * dt)
    den = a * a + w * w
    zr = abar_r - 1.0
    gr = (zr * a + abar_i * w) / den
    gi = (abar_i * a - zr * w) / den
    br, bi = b_re.astype(F32), b_im.astype(F32)
    bbar_r = gr[..., None] * br - gi[..., None] * bi
    bbar_i = gr[..., None] * bi + gi[..., None] * br
    eye = jnp.eye(groups, dtype=F32)
    n = groups * S5_STATE

    def in_mat(t):
        return jnp.einsum('gpc,gh->gchp', t, eye).reshape(groups * S5_GROUP, n)

    def out_mat(t):
        return jnp.einsum('gcp,gh->gphc', t, eye).reshape(n, groups * S5_GROUP)

    bmat = jnp.concatenate([in_mat(bbar_r), in_mat(bbar_i)], axis=1).astype(BF16)
    cmat = jnp.concatenate([out_mat(c_re.astype(F32)), out_mat(-c_im.astype(F32))], axis=0).astype(BF16)
    return abar_r.reshape(n), abar_i.reshape(n), bmat, cmat


def _s5(u2, bmat_f, bmat_b, cmat_f, cmat_b, avec, steps):
    s, width = u2.shape
    cw = width // SUBLANES
    nstate = avec.shape[-1]
    n = s // steps
    tr = steps * SUBLANES
    const = lambda shape: pl.BlockSpec(shape, lambda i: (0,) * len(shape))
    return pl.pallas_call(
        functools.partial(_s5_kernel, steps=steps, nstate=nstate, cw=cw),
        grid=(n,),
        in_specs=[pl.BlockSpec((steps, width), lambda i: (i, 0)),
                  pl.BlockSpec((steps, width), lambda i: (n - 1 - i, 0)),
                  const(bmat_f.shape), const(bmat_b.shape), const(cmat_f.shape), const(cmat_b.shape),
                  const(avec.shape)],
        out_specs=[pl.BlockSpec((steps, width), lambda i: (i, 0)),
                   pl.BlockSpec((steps, width), lambda i: (n - 1 - i, 0))],
        out_shape=[jax.ShapeDtypeStruct(u2.shape, F32), jax.ShapeDtypeStruct(u2.shape, F32)],
        scratch_shapes=[pltpu.VMEM((tr, 2 * nstate), F32), pltpu.VMEM((tr, 2 * nstate), F32),
                        pltpu.VMEM((cw // LANES, tr, LANES), F32), pltpu.VMEM((4, SUBLANES, nstate), F32)],
        compiler_params=_params(("arbitrary",)),
        name="s5_scan",
    )(u2, u2, bmat_f, bmat_b, cmat_f, cmat_b, avec)


def _out_proj_kernel(oa_ref, ob_ref, yf_ref, yb_ref, u_ref, x_ref,
                     ga_ref, gb_ref, gc_ref, dsk_ref, wglu_ref, wout_ref, gf_ref, wrh_ref, wrl_ref,
                     z_ref, aff_ref, *, a_pairs, b_pairs):
    oa = jnp.concatenate([oa_ref[0, p].astype(F32) for p in range(a_pairs)], axis=-1)
    ob = jnp.concatenate([ob_ref[0, p].astype(F32) for p in range(b_pairs)], axis=-1)
    y = (yf_ref[...] + yb_ref[...]) + dsk_ref[...] * u_ref[...]
    g = jax.nn.gelu(y)
    oc = g * jax.nn.sigmoid(jnp.dot(g.astype(BF16), wglu_ref[...], preferred_element_type=F32))
    mix = jnp.concatenate([_rms(oa, ga_ref[...]), _rms(ob, gb_ref[...]), _rms(oc, gc_ref[...])], axis=-1)
    x1 = x_ref[...] + jnp.dot(mix.astype(BF16), wout_ref[...], preferred_element_type=F32)
    d = x1.shape[1]
    z_ref[:, 0:d] = x1
    z_ref[:, d:2 * d] = x1
    h = _rms(x1, gf_ref[...])
    h_hi = h.astype(BF16)
    h_lo = (h - h_hi.astype(F32)).astype(BF16)
    logits = (lax.dot_general(wrh_ref[...], h_hi, _NT, preferred_element_type=F32)
              + lax.dot_general(wrh_ref[...], h_lo, _NT, preferred_element_type=F32)
              + lax.dot_general(wrl_ref[...], h_hi, _NT, preferred_element_type=F32))
    e = jnp.exp(logits - jnp.max(logits, axis=0, keepdims=True))
    aff_ref[0] = e / jnp.sum(e, axis=0, keepdims=True)


def _out_proj(oa, ob, yf, yb, u2, xz, colblk, ga, gb, gc, dsk, wglu, wout, gf, wrh, wrl, tm):
    b, a_pairs, s, _ = oa.shape
    b_pairs = ob.shape[1]
    d = wout.shape[1]
    nt = s // tm
    cw = u2.shape[1] // b
    ne = wrh.shape[0]
    src = lambda bi: jnp.minimum(bi, b - 1)
    spec_a = pl.BlockSpec((1, a_pairs, tm, LANES), lambda bi, i: (src(bi), 0, i, 0))
    spec_b = pl.BlockSpec((1, b_pairs, tm, LANES), lambda bi, i: (src(bi), 0, i, 0))
    spec_c = pl.BlockSpec((tm, cw), lambda bi, i: (i, src(bi)))
    const = lambda a: pl.BlockSpec(a.shape, lambda bi, i: (0,) * a.ndim)
    consts = (ga, gb, gc, dsk, wglu, wout, gf, wrh, wrl)
    return pl.pallas_call(
        functools.partial(_out_proj_kernel, a_pairs=a_pairs, b_pairs=b_pairs),
        grid=(b + 1, nt),
        in_specs=[spec_a, spec_b, spec_c, spec_c, spec_c,
                  pl.BlockSpec((tm, d), lambda bi, i: (src(bi) * nt + i, colblk))] + [const(a) for a in consts],
        out_specs=[pl.BlockSpec((tm, 2 * d), lambda bi, i: (bi * nt + i, 0)),
                   pl.BlockSpec((1, ne, tm), lambda bi, i: (bi, 0, i))],
        out_shape=[jax.ShapeDtypeStruct(((b + 1) * s, 2 * d), F32), jax.ShapeDtypeStruct((b + 1, ne, s), F32)],
        compiler_params=_params(("arbitrary", "arbitrary")),
        name="out_proj",
    )(oa, ob, yf, yb, u2, xz, *consts)


F32_VALUE_BITS = 31
GATE_PARTS = 3
TOK_RADIX = 64


def _select_kernel(aff_ref, posm_ref, offs_ref, *, cap, ntiles):
    a = aff_ref[0]
    ne = a.shape[0]
    bits = pltpu.bitcast(a, jnp.int32)

    def refine(i, prefix):
        cand = prefix | (jnp.int32(1) << (F32_VALUE_BITS - 1 - i))
        cnt = jnp.sum(jnp.where(bits >= cand, 1.0, 0.0), axis=1, keepdims=True)
        return jnp.where(cnt >= cap, cand, prefix)

    thr = lax.fori_loop(0, F32_VALUE_BITS, refine, jnp.zeros((ne, 1), jnp.int32))
    gt = bits > thr
    eq = bits == thr
    need = cap - jnp.sum(jnp.where(gt, 1.0, 0.0), axis=1, keepdims=True)

    row = lax.broadcasted_iota(jnp.int32, (LANES, LANES), 0)
    colm = lax.broadcasted_iota(jnp.int32, (LANES, LANES), 1)
    incl = jnp.where(row <= colm, 1.0, 0.0).astype(BF16)
    wmat = jnp.concatenate([incl, jnp.ones((LANES, LANES), BF16)], axis=1)
    lane = lax.broadcasted_iota(jnp.int32, (ne, LANES), 1)
    off_gt = jnp.zeros((ne, LANES), F32)
    off_eq = jnp.zeros((ne, LANES), F32)
    offs = jnp.zeros((ne, LANES), F32)
    for j in range(ntiles):
        sl = slice(j * LANES, (j + 1) * LANES)
        g = jnp.where(gt[:, sl], 1.0, 0.0)
        q = jnp.where(eq[:, sl], 1.0, 0.0)
        r = jnp.dot(jnp.concatenate([g, q], axis=0).astype(BF16), wmat, preferred_element_type=F32)
        before_gt = off_gt + r[:ne, :LANES] - g
        before_eq = off_eq + r[ne:, :LANES] - q
        sel = gt[:, sl] | (eq[:, sl] & (before_eq < need))
        posm_ref[0, :, sl] = jnp.where(sel, before_gt + jnp.minimum(before_eq, need), -1.0)
        offs = jnp.where(lane == j, off_gt + jnp.minimum(off_eq, need), offs)
        off_gt = off_gt + r[:ne, LANES:]
        off_eq = off_eq + r[ne:, LANES:]
    offs_ref[0] = offs.astype(jnp.int32)


def _compact_kernel(offs_ref, posm_ref, tok_ref, idx_ref, gate_ref, acc_ref, *, cap, ntiles):
    b = pl.program_id(0)
    ne = posm_ref.shape[1]
    win = 2 * LANES
    acc_ref[...] = jnp.zeros_like(acc_ref)
    srow = lax.broadcasted_iota(jnp.int32, (win, 1), 0).astype(F32)

    def tile(j, carry):
        t0 = pl.multiple_of(j * LANES, LANES)
        pos = posm_ref[0, :, pl.ds(t0, LANES)]
        bases, onehots = [], []
        for e in range(ne):
            base = pl.multiple_of((offs_ref[(b * ne + e) * LANES + j] // LANES) * LANES, LANES)
            rel = pos[e:e + 1, :] - base.astype(F32)
            onehots.append(jnp.where(rel == srow, 1.0, 0.0).astype(BF16))
            bases.append(base)
        r = jnp.dot(jnp.concatenate(onehots, axis=0), tok_ref[0, pl.ds(t0, LANES), :],
                    preferred_element_type=F32)
        for e in range(ne):
            rows = pl.ds(bases[e], win)
            acc_ref[e, rows, :] = acc_ref[e, rows, :] + r[e * win:(e + 1) * win]
        return carry

    lax.fori_loop(0, ntiles, tile, 0)
    for e in range(ne):
        a = acc_ref[e, 0:cap, :]
        idx_ref[0, e] = (a[:, 0:1] * TOK_RADIX + a[:, 1:2]).astype(jnp.int32)
        gate = a[:, 2 + e:3 + e]
        for part in range(1, GATE_PARTS):
            gate = gate + a[:, 2 + part * ne + e:3 + part * ne + e]
        gate_ref[0, e] = gate


def _token_table(aff):
    b, ne, s = aff.shape
    t = jnp.arange(s, dtype=jnp.int32)
    assert s <= TOK_RADIX * 256
    cols = [jnp.broadcast_to((t // TOK_RADIX).astype(BF16)[None, :, None], (b, s, 1)),
            jnp.broadcast_to((t % TOK_RADIX).astype(BF16)[None, :, None], (b, s, 1))]
    rest = jnp.swapaxes(aff, 1, 2)
    for _ in range(GATE_PARTS):
        piece = rest.astype(BF16)
        cols.append(piece)
        rest = rest - piece.astype(F32)
    used = 2 + GATE_PARTS * ne
    assert used <= LANES
    cols.append(jnp.zeros((b, s, LANES - used), BF16))
    return jnp.concatenate(cols, axis=-1)


def _route(aff, cap):
    b, ne, s = aff.shape
    ntiles = s // LANES
    assert ntiles <= LANES and s % LANES == 0
    row_spec = pl.BlockSpec((1, ne, s), lambda bi: (bi, 0, 0))
    posm, offs = pl.pallas_call(
        functools.partial(_select_kernel, cap=cap, ntiles=ntiles),
        grid=(b,),
        in_specs=[row_spec],
        out_specs=[row_spec, pl.BlockSpec((1, ne, LANES), lambda bi: (bi, 0, 0))],
        out_shape=[jax.ShapeDtypeStruct((b, ne, s), F32), jax.ShapeDtypeStruct((b, ne, LANES), jnp.int32)],
        compiler_params=_params(("parallel",)),
        name="route_select",
    )(aff)
    slot_spec = pl.BlockSpec((1, ne, cap, 1), lambda bi, offs: (bi, 0, 0, 0))
    idx, gate = pl.pallas_call(
        functools.partial(_compact_kernel, cap=cap, ntiles=ntiles),
        grid_spec=pltpu.PrefetchScalarGridSpec(
            num_scalar_prefetch=1,
            grid=(b,),
            in_specs=[pl.BlockSpec((1, ne, s), lambda bi, offs: (bi, 0, 0)),
                      pl.BlockSpec((1, s, LANES), lambda bi, offs: (bi, 0, 0))],
            out_specs=[slot_spec, slot_spec],
            scratch_shapes=[pltpu.VMEM((ne, cap + 2 * LANES, LANES), F32)]),
        out_shape=[jax.ShapeDtypeStruct((b, ne, cap, 1), jnp.int32), jax.ShapeDtypeStruct((b, ne, cap, 1), F32)],
        compiler_params=_params(("parallel",)),
        name="route_compact",
    )(offs.reshape(-1), posm, _token_table(aff))
    return idx, gate


ROW_UNROLL = 8


def _moe_kernel(idx_ref, gate_ref, gf_ref, wg_ref, wu_ref, wd_ref, z_in_ref, z_ref,
                gbuf, obuf, hbuf, gsem, ssem, *, ne, nb, cap, seq, d, fchunk):
    del z_in_ref
    e, b = pl.program_id(0), pl.program_id(1)
    n = e * nb + b
    slot = n % 2
    other = 1 - slot

    def rows_of(step):
        e2, b2 = step // nb, step % nb
        return (b2 * ne + e2) * cap, b2 * seq

    def row_copies(step, fn):
        base, tok0 = rows_of(step)

        def body(i, carry):
            for k in range(ROW_UNROLL):
                r = i * ROW_UNROLL + k
                fn(r, idx_ref[base + r] + tok0)
            return carry

        lax.fori_loop(0, cap // ROW_UNROLL, body, 0)

    def gather_copy(sl, r, tok):
        return pltpu.make_async_copy(z_ref.at[pl.ds(tok, 1), :], gbuf.at[sl, pl.ds(r, 1), :], gsem.at[sl])

    def scatter_copy(sl, r, tok):
        return pltpu.make_async_copy(obuf.at[sl, pl.ds(r, 1), :], z_ref.at[pl.ds(tok, 1), pl.ds(d, d)], ssem)

    def wait_scatter(sl):
        pltpu.make_async_copy(obuf.at[sl], z_ref.at[pl.ds(0, cap), pl.ds(d, d)], ssem).wait()

    def wait_gather(sl):
        pltpu.make_async_copy(z_ref.at[pl.ds(0, cap), :], gbuf.at[sl], gsem.at[sl]).wait()

    total = ne * nb
    first, last = n == 0, n == total - 1

    @pl.when(first)
    def _():
        row_copies(0, lambda r, tok: gather_copy(0, r, tok).start())
        obuf[1] = jnp.zeros((cap, d), F32)

    wait_gather(slot)
    hbuf[...] = _rms(gbuf[slot, :, 0:d], gf_ref[...]).astype(BF16)
    obuf[slot] = jnp.zeros((cap, d), F32)
    nf = wg_ref.shape[2] // fchunk
    per = cap // nf

    base_p, tok0_p = rows_of(jnp.maximum(n - 1, 0))
    tok0_p = jnp.where(first, nb * seq, tok0_p)
    base_n, tok0_n = rows_of(jnp.minimum(n + 1, total - 1))
    for i in range(nf):
        fs = slice(i * fchunk, (i + 1) * fchunk)
        h = hbuf[...]
        g = jnp.dot(h, wg_ref[0, :, fs], preferred_element_type=F32)
        up = jnp.dot(h, wu_ref[0, :, fs], preferred_element_type=F32)
        hid = (jax.nn.silu(g) * up).astype(BF16)
        obuf[slot] = obuf[slot] + jnp.dot(hid, wd_ref[0, fs, :], preferred_element_type=F32)
        for r in range(i * per, (i + 1) * per):
            gather_copy(other, r, idx_ref[base_n + r] + tok0_n).start()
            scatter_copy(other, r, idx_ref[base_p + r] + tok0_p).start()
    wait_scatter(other)
    obuf[slot] = gbuf[slot, :, d:2 * d] + obuf[slot] * gate_ref[0, 0]

    @pl.when(last)
    def _():
        wait_gather(other)
        row_copies(n, lambda r, tok: scatter_copy(slot, r, tok).start())
        wait_scatter(slot)


def _moe(idx, gate, z, gf, wg, wu, wd, seq, fchunk=256):
    b, ne, cap, _ = idx.shape
    d, ff = wg.shape[1], wg.shape[2]
    fchunk = min(fchunk, ff)
    assert cap % ROW_UNROLL == 0 and ff % fchunk == 0 and cap % (ff // fchunk) == 0
    assert b >= 3 and z.shape[0] == (b + 1) * seq, "z carries one pad sequence after the real ones"
    once = pl.Buffered(1)
    grid_spec = pltpu.PrefetchScalarGridSpec(
        num_scalar_prefetch=1,
        grid=(ne, b),
        in_specs=[pl.BlockSpec((1, 1, cap, 1), lambda e, bi, idx: (bi, e, 0, 0)),
                  pl.BlockSpec((1, d), lambda e, bi, idx: (0, 0)),
                  pl.BlockSpec((1, d, ff), lambda e, bi, idx: (e, 0, 0), pipeline_mode=once),
                  pl.BlockSpec((1, d, ff), lambda e, bi, idx: (e, 0, 0), pipeline_mode=once),
                  pl.BlockSpec((1, ff, d), lambda e, bi, idx: (e, 0, 0), pipeline_mode=once),
                  pl.BlockSpec(memory_space=pl.ANY)],
        out_specs=pl.BlockSpec(memory_space=pl.ANY),
        scratch_shapes=[pltpu.VMEM((2, cap, 2 * d), F32), pltpu.VMEM((2, cap, d), F32), pltpu.VMEM((cap, d), BF16),
                        pltpu.SemaphoreType.DMA((2,)), pltpu.SemaphoreType.DMA(())],
    )
    return pl.pallas_call(
        functools.partial(_moe_kernel, ne=ne, nb=b, cap=cap, seq=seq, d=d, fchunk=fchunk),
        grid_spec=grid_spec,
        out_shape=jax.ShapeDtypeStruct(z.shape, z.dtype),
        input_output_aliases={6: 0},
        compiler_params=_params(("arbitrary", "arbitrary"), disable_bounds_checks=True),
        name="moe",
    )(idx.reshape(-1), gate, gf, wg, wu, wd, z)


def _tile2(g):
    return jnp.concatenate([g, g], axis=-1)


def kernel(x, attn_norm, w_in, q_norm_a, k_norm_a, q_norm_b, k_norm_b, rel_pos_bias, s5_a_re, s5_a_im, s5_log_dt, s5_b_re, s5_b_im, s5_c_re, s5_c_im, s5_d, w_glu, out_norm_a, out_norm_b, out_norm_c, w_out, ffn_norm, w_router, w_gate, w_up, w_down):
    b, s, d = x.shape
    depth = w_in.shape[0]
    assert b == SUBLANES, "the S5 scan keeps one sequence per sublane"
    tm = min(512, s)
    steps = min(64, s)
    cap = EC_CAPACITY * s // N_EXPERTS
    scale = HEAD_DIM ** -0.5
    nd = len(DILS)

    w_in_b, w_out_b, w_glu_b = w_in.astype(BF16), w_out.astype(BF16), w_glu.astype(BF16)
    w_gate_b, w_up_b, w_down_b = w_gate.astype(BF16), w_up.astype(BF16), w_down.astype(BF16)
    wr_t = jnp.swapaxes(w_router, 1, 2)
    wr_hi = wr_t.astype(BF16)
    wr_lo = (wr_t - wr_hi.astype(F32)).astype(BF16)

    xz, colblk = x.reshape(b * s, d), 0
    for l in range(depth):
        hg = jnp.stack([_tile2(q_norm_a[l]) * scale, _tile2(k_norm_a[l]),
                        _tile2(q_norm_b[l]) * scale, _tile2(k_norm_b[l])], axis=0)
        outs = _in_proj(xz, colblk, b, s, attn_norm[l][None], w_in_b[l], hg, tm)
        qa, ka, va = outs[0:nd], outs[nd:2 * nd], outs[2 * nd:3 * nd]
        qb, kb, vb, u2 = outs[3 * nd:]

        oa = _dilated(qa, ka, va)
        ob = _natten(qb, kb, vb, _na_bias(rel_pos_bias[l]))

        pf = _s5_params(s5_a_re[l, 0], s5_a_im[l, 0], s5_log_dt[l, 0], s5_b_re[l, 0], s5_b_im[l, 0],
                        s5_c_re[l, 0], s5_c_im[l, 0])
        pb = _s5_params(s5_a_re[l, 1], s5_a_im[l, 1], s5_log_dt[l, 1], s5_b_re[l, 1], s5_b_im[l, 1],
                        s5_c_re[l, 1], s5_c_im[l, 1])
        avec = jnp.broadcast_to(jnp.stack([pf[0], pf[1], pb[0], pb[1]])[:, None, :],
                                (4, SUBLANES, pf[0].shape[0]))
        yf, yb = _s5(u2, pf[2], pb[2], pf[3], pb[3], avec, steps)

        z, aff = _out_proj(oa, ob, yf, yb, u2, xz, colblk,
                           out_norm_a[l][None], out_norm_b[l][None], out_norm_c[l][None], s5_d[l][None],
                           w_glu_b[l], w_out_b[l], ffn_norm[l][None], wr_hi[l], wr_lo[l], tm)

        idx, gate = _route(aff[:b], cap)
        xz = _moe(idx, gate, z, ffn_norm[l][None], w_gate_b[l], w_up_b[l], w_down_b[l], s)
        colblk = 1
    return xz[:b * s, d:].reshape(b, s, d)
```

```python
import functools

import numpy as np
import jax
import jax.numpy as jnp
from jax import lax
from jax.experimental import pallas as pl
from jax.experimental.pallas import tpu as pltpu

HEAD_DIM = 64
A_HEADS = 8
B_HEADS = 4
DILATED_CONFIGS = ((128, 1), (512, 4), (2048, 16))
GRID_W = 64
NA_ROWS = 8
NA_COLS = 16
S5_GROUP = 16
S5_STATE = 64
N_EXPERTS = 16
EC_CAPACITY = 2
EPS = 1e-6
NEG_INF = -1e30

LANES = 128
SUBLANES = 8
VMEM_LIMIT = 56 * 1024 * 1024

HALF = 64
QBLK = 2 * HALF
KBLK = 4 * HALF
DIL_CHUNK = 2048
DILS = tuple(d for _, d in DILATED_CONFIGS)
assert all(w == 2 * HALF * d for w, d in DILATED_CONFIGS) and DILS[0] == 1

F32 = jnp.float32
BF16 = jnp.bfloat16

_NT = (((1,), (1,)), ((), ()))


def _params(sem, vmem=VMEM_LIMIT, **kw):
    return pltpu.CompilerParams(dimension_semantics=sem, vmem_limit_bytes=vmem, **kw)


def _lo_lanes():
    return lax.broadcasted_iota(jnp.int32, (1, LANES), 1) < HEAD_DIM


def _stack_heads(q2, lo):
    zero = jnp.zeros_like(q2)
    return jnp.concatenate([jnp.where(lo, q2, zero), jnp.where(lo, zero, q2)], axis=0)


def _rms(t, gain):
    return (t * lax.rsqrt(jnp.mean(t * t, axis=-1, keepdims=True) + EPS)) * gain


def _in_proj_kernel(x_ref, g_ref, w_ref, hg_ref, *rest, a_pairs, b_pairs, tm, nsplit):
    nd = len(DILS)
    qa_refs, ka_refs, va_refs = rest[0:nd], rest[nd:2 * nd], rest[2 * nd:3 * nd]
    qb_ref, kb_ref, vb_ref, u_ref, scr = rest[3 * nd:]
    lo = _lo_lanes()

    def head_norm(c, gain_row):
        if gain_row is None:
            return c
        sq = c * c
        s_lo = jnp.sum(jnp.where(lo, sq, 0.0), axis=-1, keepdims=True)
        s_hi = jnp.sum(jnp.where(lo, 0.0, sq), axis=-1, keepdims=True)
        r = jnp.where(lo, lax.rsqrt(s_lo / HEAD_DIM + EPS), lax.rsqrt(s_hi / HEAD_DIM + EPS))
        return (c * r) * hg_ref[gain_row:gain_row + 1, :]

    sub = tm // nsplit
    for st in range(nsplit):
        rows = slice(st * sub, (st + 1) * sub)
        h = _rms(x_ref[rows, :], g_ref[...])
        proj = jnp.dot(h.astype(BF16), w_ref[...], preferred_element_type=F32)
        col = 0
        nscr = 0
        for refs, gain_row in ((qa_refs, 0), (ka_refs, 1), (va_refs, None)):
            for p in range(a_pairs):
                c = head_norm(proj[:, col:col + LANES], gain_row)
                refs[0][0, p, rows, :] = c.astype(BF16)
                scr[nscr, rows, :] = c
                for ref, dil in zip(refs[1:], DILS[1:]):
                    vrows = slice(st * sub // dil, (st + 1) * sub // dil)
                    for r in range(dil):
                        ref[0, p, vrows, r * LANES:(r + 1) * LANES] = (
                            scr[nscr, pl.ds(st * sub + r, sub // dil, stride=dil), :].astype(BF16))
                nscr += 1
                col += LANES
        for ref, gain_row in ((qb_ref, 2), (kb_ref, 3), (vb_ref, None)):
            for p in range(b_pairs):
                ref[0, p, rows, :] = head_norm(proj[:, col:col + LANES], gain_row).astype(BF16)
                col += LANES
        u_ref[rows, :] = proj[:, col:]


def _in_proj(xz, colblk, b, s, g, w, hg, tm, nsplit=1):
    assert tm % (nsplit * 2 * SUBLANES * DILS[-1]) == 0
    d = g.shape[1]
    nt = s // tm
    a_pairs, b_pairs = A_HEADS // 2, B_HEADS // 2
    cw = w.shape[1] - 3 * (a_pairs + b_pairs) * LANES
    shapes_a = [jax.ShapeDtypeStruct((b, a_pairs, s // dil, dil * LANES), BF16) for dil in DILS]
    specs_a = [pl.BlockSpec((1, a_pairs, tm // dil, dil * LANES), lambda bi, i: (bi, 0, i, 0)) for dil in DILS]
    shape_b = jax.ShapeDtypeStruct((b, b_pairs, s, LANES), BF16)
    spec_b = pl.BlockSpec((1, b_pairs, tm, LANES), lambda bi, i: (bi, 0, i, 0))
    return pl.pallas_call(
        functools.partial(_in_proj_kernel, a_pairs=a_pairs, b_pairs=b_pairs, tm=tm, nsplit=nsplit),
        grid=(b, nt),
        in_specs=[pl.BlockSpec((tm, d), lambda bi, i: (bi * nt + i, colblk)),
                  pl.BlockSpec((1, d), lambda bi, i: (0, 0)),
                  pl.BlockSpec(w.shape, lambda bi, i: (0, 0)),
                  pl.BlockSpec(hg.shape, lambda bi, i: (0, 0))],
        out_specs=specs_a * 3 + [spec_b] * 3 + [pl.BlockSpec((tm, cw), lambda bi, i: (i, bi))],
        out_shape=shapes_a * 3 + [shape_b] * 3 + [jax.ShapeDtypeStruct((s, b * cw), F32)],
        scratch_shapes=[pltpu.VMEM((3 * a_pairs, tm, LANES), F32)],
        compiler_params=_params(("parallel", "parallel")),
        name="in_proj",
    )(xz, g, w, hg)


def _dilated_kernel(*refs, chunk):
    nd = len(DILS)
    o_ref, acc_ref, lse_ref = refs[8 * nd:]
    c = pl.program_id(2)
    nc = pl.num_programs(2)
    lo = _lo_lanes()
    kcol = lax.broadcasted_iota(jnp.int32, (1, KBLK), 1)
    edge_first = jnp.where((kcol < HALF) & (c == 0), NEG_INF, 0.0)
    edge_last = jnp.where((kcol >= KBLK - HALF) & (c == nc - 1), NEG_INF, 0.0)
    for bi, dil in enumerate(DILS):
        q_ref, kp_ref, k_ref, kn_ref, vp_ref, v_ref, vn_ref, bias_ref = refs[8 * bi:8 * bi + 8]
        nblk = chunk // dil // QBLK
        bias = bias_ref[0]
        for r in range(dil):
            cs = slice(r * LANES, (r + 1) * LANES)
            kc = jnp.concatenate([kp_ref[0, 0, :, cs], k_ref[0, 0, :, cs], kn_ref[0, 0, :, cs]], axis=0)
            vc = jnp.concatenate([vp_ref[0, 0, :, cs], v_ref[0, 0, :, cs], vn_ref[0, 0, :, cs]], axis=0)
            for m in range(nblk):
                qs = _stack_heads(q_ref[0, 0, m * QBLK:(m + 1) * QBLK, cs], lo)
                kk = kc[m * QBLK:m * QBLK + KBLK]
                vv = vc[m * QBLK:m * QBLK + KBLK]
                s = lax.dot_general(qs, kk, _NT, preferred_element_type=F32) + bias
                if m == 0:
                    s = s + edge_first
                if m == nblk - 1:
                    s = s + edge_last
                mx = jnp.max(s, axis=-1, keepdims=True)
                p = jnp.exp(s - mx)
                den = jnp.sum(p, axis=-1, keepdims=True)
                o = jnp.dot(p.astype(BF16), vv, preferred_element_type=F32) / den
                lse = mx + jnp.log(den)
                o2 = jnp.where(lo, o[:QBLK], o[QBLK:])
                l2 = jnp.where(lo, lse[:QBLK], lse[QBLK:])
                tok = pl.ds(m * QBLK * dil + r, QBLK, stride=dil) if dil > 1 else pl.ds(m * QBLK, QBLK)
                if bi == 0:
                    acc_ref[tok, :] = o2
                    lse_ref[tok, :] = l2
                else:
                    a_old, l_old = acc_ref[tok, :], lse_ref[tok, :]
                    m2 = jnp.maximum(l_old, l2)
                    e_old, e_new = jnp.exp(l_old - m2), jnp.exp(l2 - m2)
                    tot = e_old + e_new
                    acc_ref[tok, :] = (a_old * e_old + o2 * e_new) / tot
                    if bi < nd - 1:
                        lse_ref[tok, :] = m2 + jnp.log(tot)
    o_ref[0, 0] = acc_ref[...].astype(BF16)


def _alibi_slopes(n):
    return np.array([2.0 ** (-8.0 * (i + 1) / n) for i in range(n)], dtype=np.float32)


def _dilated_bias(dil):
    rel = (np.arange(KBLK)[None, :] - HALF) - np.arange(QBLK)[:, None]
    valid = np.abs(rel) <= HALF
    dist = (dil * np.abs(rel)).astype(np.float32)
    slopes = _alibi_slopes(A_HEADS)
    per_head = np.where(valid[None], -(slopes[:, None, None] * dist[None]), np.float32(NEG_INF)).astype(np.float32)
    return per_head.reshape(A_HEADS // 2, 2 * QBLK, KBLK)


def _dilated(q_views, k_views, v_views):
    b, hp, s, _ = q_views[0].shape
    chunk = min(DIL_CHUNK, s)
    assert s % chunk == 0 and chunk % (QBLK * DILS[-1]) == 0
    args, specs = [], []
    for q, k, v, dil in zip(q_views, k_views, v_views, DILS):
        rows, width = chunk // dil, dil * LANES
        hb = rows // HALF
        last = s // dil // HALF - 1
        main = pl.BlockSpec((1, 1, rows, width), lambda bi, h, c: (bi, h, c, 0))
        prev = pl.BlockSpec((1, 1, HALF, width), lambda bi, h, c, hb=hb: (bi, h, jnp.maximum(c * hb - 1, 0), 0))
        nxt = pl.BlockSpec((1, 1, HALF, width),
                           lambda bi, h, c, hb=hb, last=last: (bi, h, jnp.minimum((c + 1) * hb, last), 0))
        args += [q, k, k, k, v, v, v, jnp.asarray(_dilated_bias(dil))]
        specs += [main, prev, main, nxt, prev, main, nxt,
                  pl.BlockSpec((1, 2 * QBLK, KBLK), lambda bi, h, c: (h, 0, 0))]
    return pl.pallas_call(
        functools.partial(_dilated_kernel, chunk=chunk),
        grid=(b, hp, s // chunk),
        in_specs=specs,
        out_specs=pl.BlockSpec((1, 1, chunk, LANES), lambda bi, h, c: (bi, h, c, 0)),
        out_shape=jax.ShapeDtypeStruct((b, hp, s, LANES), BF16),
        scratch_shapes=[pltpu.VMEM((chunk, LANES), F32), pltpu.VMEM((chunk, LANES), F32)],
        compiler_params=_params(("parallel", "parallel", "parallel")),
        name="dilated",
    )(*args)


def _na_kernel(q_ref, k_ref, v_ref, bias_ref, o_ref, *, rb, nrows):
    i = pl.program_id(2)
    lo = _lo_lanes()
    kspan = NA_ROWS * GRID_W
    scores, values = [], []
    for j in range(rb):
        r = i * rb + j
        rs = jnp.clip(r - NA_ROWS // 2, 0, nrows - NA_ROWS)
        start = pl.multiple_of(rs * GRID_W, GRID_W)
        kk = k_ref[0, 0, pl.ds(start, kspan), :]
        values.append(v_ref[0, 0, pl.ds(start, kspan), :])
        qs = _stack_heads(q_ref[0, 0, j * GRID_W:(j + 1) * GRID_W, :], lo)
        scores.append(lax.dot_general(qs, kk, _NT, preferred_element_type=F32) + bias_ref[0, r - rs])
    probs, dens = [], []
    for s in scores:
        p = jnp.exp(s - jnp.max(s, axis=-1, keepdims=True))
        dens.append(jnp.sum(p, axis=-1, keepdims=True))
        probs.append(p.astype(BF16))
    for j in range(rb):
        o = jnp.dot(probs[j], values[j], preferred_element_type=F32) / dens[j]
        o_ref[0, 0, j * GRID_W:(j + 1) * GRID_W, :] = jnp.where(lo, o[:GRID_W], o[GRID_W:]).astype(BF16)


def _na_bias(rpb):
    t = np.arange(NA_ROWS, dtype=np.int32)[:, None, None, None]
    krow = np.arange(NA_ROWS, dtype=np.int32)[None, None, :, None]
    qc = np.arange(GRID_W, dtype=np.int32)[None, :, None, None]
    kc = np.arange(GRID_W, dtype=np.int32)[None, None, None, :]
    dr = np.broadcast_to(krow - t + NA_ROWS - 1, (NA_ROWS, GRID_W, NA_ROWS, GRID_W))
    dc = np.broadcast_to(np.clip(kc - qc + NA_COLS - 1, 0, 2 * NA_COLS - 2), dr.shape)
    cs = np.clip(qc - NA_COLS // 2, 0, GRID_W - NA_COLS)
    valid = np.broadcast_to((kc >= cs) & (kc < cs + NA_COLS), dr.shape)
    row_sel = np.equal(dr[:, 0, :, 0][..., None], np.arange(2 * NA_ROWS - 1)).astype(np.float32)
    col_sel = np.equal(dc[0, :, 0, :][..., None], np.arange(2 * NA_COLS - 1)).astype(np.float32)
    tab = jnp.einsum('hab,qcb->haqc', rpb.astype(F32), col_sel, precision=lax.Precision.HIGHEST)
    tab = jnp.einsum('haqc,tka->htqkc', tab, row_sel, precision=lax.Precision.HIGHEST)
    tab = jnp.where(valid[None], tab, NEG_INF).reshape(B_HEADS // 2, 2, NA_ROWS, GRID_W, NA_ROWS * GRID_W)
    return tab.transpose(0, 2, 1, 3, 4).reshape(B_HEADS // 2, NA_ROWS, 2 * GRID_W, NA_ROWS * GRID_W)


def _natten(q, k, v, bias, rb=16):
    b, hp, s, _ = q.shape
    nrows = s // GRID_W
    assert nrows >= NA_ROWS and nrows % rb == 0
    full = pl.BlockSpec((1, 1, s, LANES), lambda bi, h, i: (bi, h, 0, 0))
    tile = pl.BlockSpec((1, 1, rb * GRID_W, LANES), lambda bi, h, i: (bi, h, i, 0))
    return pl.pallas_call(
        functools.partial(_na_kernel, rb=rb, nrows=nrows),
        grid=(b, hp, nrows // rb),
        in_specs=[tile, full, full,
                  pl.BlockSpec((1,) + bias.shape[1:], lambda bi, h, i: (h, 0, 0, 0))],
        out_specs=tile,
        out_shape=jax.ShapeDtypeStruct(q.shape, BF16),
        compiler_params=_params(("parallel", "parallel", "arbitrary")),
        name="natten",
    )(q, k, v, bias)


def _s5_kernel(uf_ref, ub_ref, bf_ref, bb_ref, cf_ref, cb_ref, a_ref, yf_ref, yb_ref,
               xf_ref, xb_ref, io_ref, st_ref, *, steps, nstate, cw):
    @pl.when(pl.program_id(0) == 0)
    def _():
        st_ref[...] = jnp.zeros_like(st_ref)

    nlt = cw // LANES

    def time_major(u_ref):
        for bi in range(SUBLANES):
            for j in range(nlt):
                col = bi * cw + j * LANES
                io_ref[j, pl.ds(bi, steps, stride=SUBLANES), :] = u_ref[:, col:col + LANES]
        return jnp.concatenate([io_ref[j] for j in range(nlt)], axis=-1).astype(BF16)

    xf_ref[...] = jnp.dot(time_major(uf_ref), bf_ref[...], preferred_element_type=F32)
    xb_ref[...] = jnp.dot(time_major(ub_ref), bb_ref[...], preferred_element_type=F32)
    re, im = slice(0, nstate), slice(nstate, 2 * nstate)

    def step(t, carry):
        xrf, xif, xrb, xib = carry
        rf = pl.ds(pl.multiple_of(t * SUBLANES, SUBLANES), SUBLANES)
        rb = pl.ds(pl.multiple_of((steps - 1 - t) * SUBLANES, SUBLANES), SUBLANES)
        nrf = a_ref[0] * xrf - a_ref[1] * xif + xf_ref[rf, re]
        nif = a_ref[0] * xif + a_ref[1] * xrf + xf_ref[rf, im]
        nrb = a_ref[2] * xrb - a_ref[3] * xib + xb_ref[rb, re]
        nib = a_ref[2] * xib + a_ref[3] * xrb + xb_ref[rb, im]
        xf_ref[rf, re] = nrf
        xf_ref[rf, im] = nif
        xb_ref[rb, re] = nrb
        xb_ref[rb, im] = nib
        return nrf, nif, nrb, nib

    carry = lax.fori_loop(0, steps, step, (st_ref[0], st_ref[1], st_ref[2], st_ref[3]))
    for n, val in enumerate(carry):
        st_ref[n] = val

    def batch_major(x_ref, c_ref, y_ref):
        y = jnp.dot(x_ref[...].astype(BF16), c_ref[...], preferred_element_type=F32)
        for j in range(nlt):
            io_ref[j] = y[:, j * LANES:(j + 1) * LANES]
        for bi in range(SUBLANES):
            for j in range(nlt):
                col = bi * cw + j * LANES
                y_ref[:, col:col + LANES] = io_ref[j, pl.ds(bi, steps, stride=SUBLANES), :]

    batch_major(xf_ref, cf_ref, yf_ref)
    batch_major(xb_ref, cb_ref, yb_ref)


def _s5_params(a_re, a_im, log_dt, b_re, b_im, c_re, c_im):
    groups = a_re.shape[0]
    dt = jnp.exp(log_dt.astype(F32))[:, None]
    a = jnp.minimum(a_re.astype(F32), -1e-4)
    w = a_im.astype(F32)
    mag = jnp.exp(a * dt)
    abar_r, abar_i = mag * jnp.cos(w * dt), mag * jnp.sin(w * dt)
    den = a * a + w * w
    zr = abar_r - 1.0
    gr = (zr * a + abar_i * w) / den
    gi = (abar_i * a - zr * w) / den
    br, bi = b_re.astype(F32), b_im.astype(F32)
    bbar_r = gr[..., None] * br - gi[..., None] * bi
    bbar_i = gr[..., None] * bi + gi[..., None] * br
    eye = jnp.eye(groups, dtype=F32)
    n = groups * S5_STATE

    def in_mat(t):
        return jnp.einsum('gpc,gh->gchp', t, eye).reshape(groups * S5_GROUP, n)

    def out_mat(t):
        return jnp.einsum('gcp,gh->gphc', t, eye).reshape(n, groups * S5_GROUP)

    bmat = jnp.concatenate([in_mat(bbar_r), in_mat(bbar_i)], axis=1).astype(BF16)
    cmat = jnp.concatenate([out_mat(c_re.astype(F32)), out_mat(-c_im.astype(F32))], axis=0).astype(BF16)
    return abar_r.reshape(n), abar_i.reshape(n), bmat, cmat


def _s5(u2, bmat_f, bmat_b, cmat_f, cmat_b, avec, steps):
    s, width = u2.shape
    cw = width // SUBLANES
    nstate = avec.shape[-1]
    n = s // steps
    tr = steps * SUBLANES
    const = lambda shape: pl.BlockSpec(shape, lambda i: (0,) * len(shape))
    return pl.pallas_call(
        functools.partial(_s5_kernel, steps=steps, nstate=nstate, cw=cw),
        grid=(n,),
        in_specs=[pl.BlockSpec((steps, width), lambda i: (i, 0)),
                  pl.BlockSpec((steps, width), lambda i: (n - 1 - i, 0)),
                  const(bmat_f.shape), const(bmat_b.shape), const(cmat_f.shape), const(cmat_b.shape),
                  const(avec.shape)],
        out_specs=[pl.BlockSpec((steps, width), lambda i: (i, 0)),
                   pl.BlockSpec((steps, width), lambda i: (n - 1 - i, 0))],
        out_shape=[jax.ShapeDtypeStruct(u2.shape, F32), jax.ShapeDtypeStruct(u2.shape, F32)],
        scratch_shapes=[pltpu.VMEM((tr, 2 * nstate), F32), pltpu.VMEM((tr, 2 * nstate), F32),
                        pltpu.VMEM((cw // LANES, tr, LANES), F32), pltpu.VMEM((4, SUBLANES, nstate), F32)],
        compiler_params=_params(("arbitrary",)),
        name="s5_scan",
    )(u2, u2, bmat_f, bmat_b, cmat_f, cmat_b, avec)


def _out_proj_kernel(*refs, a_pairs, b_pairs, nb):
    z_ref, aff_ref = refs[-2:]
    real = pl.program_id(0) < nb
    pl.when(real)(lambda: _out_proj_body(*refs, a_pairs=a_pairs, b_pairs=b_pairs))

    @pl.when(jnp.logical_not(real))
    def _():
        z_ref[...] = jnp.zeros_like(z_ref)
        aff_ref[...] = jnp.zeros_like(aff_ref)


def _out_proj_body(oa_ref, ob_ref, yf_ref, yb_ref, u_ref, x_ref,
                   ga_ref, gb_ref, gc_ref, dsk_ref, wglu_ref, wout_ref, gf_ref, wrh_ref, wrl_ref,
                   z_ref, aff_ref, *, a_pairs, b_pairs):
    oa = jnp.concatenate([oa_ref[0, p].astype(F32) for p in range(a_pairs)], axis=-1)
    ob = jnp.concatenate([ob_ref[0, p].astype(F32) for p in range(b_pairs)], axis=-1)
    y = (yf_ref[...] + yb_ref[...]) + dsk_ref[...] * u_ref[...]
    g = jax.nn.gelu(y)
    oc = g * jax.nn.sigmoid(jnp.dot(g.astype(BF16), wglu_ref[...], preferred_element_type=F32))
    mix = jnp.concatenate([_rms(oa, ga_ref[...]), _rms(ob, gb_ref[...]), _rms(oc, gc_ref[...])], axis=-1)
    x1 = x_ref[...] + jnp.dot(mix.astype(BF16), wout_ref[...], preferred_element_type=F32)
    d = x1.shape[1]
    z_ref[:, 0:d] = x1
    z_ref[:, d:2 * d] = x1
    h = _rms(x1, gf_ref[...])
    h_hi = h.astype(BF16)
    h_lo = (h - h_hi.astype(F32)).astype(BF16)
    logits = (lax.dot_general(wrh_ref[...], h_hi, _NT, preferred_element_type=F32)
              + lax.dot_general(wrh_ref[...], h_lo, _NT, preferred_element_type=F32)
              + lax.dot_general(wrl_ref[...], h_hi, _NT, preferred_element_type=F32))
    e = jnp.exp(logits - jnp.max(logits, axis=0, keepdims=True))
    aff_ref[0] = e / jnp.sum(e, axis=0, keepdims=True)


def _out_proj(oa, ob, yf, yb, u2, xz, colblk, ga, gb, gc, dsk, wglu, wout, gf, wrh, wrl, tm):
    b, a_pairs, s, _ = oa.shape
    b_pairs = ob.shape[1]
    d = wout.shape[1]
    nt = s // tm
    cw = u2.shape[1] // b
    ne = wrh.shape[0]
    src = lambda bi: jnp.minimum(bi, b - 1)
    spec_a = pl.BlockSpec((1, a_pairs, tm, LANES), lambda bi, i: (src(bi), 0, i, 0))
    spec_b = pl.BlockSpec((1, b_pairs, tm, LANES), lambda bi, i: (src(bi), 0, i, 0))
    spec_c = pl.BlockSpec((tm, cw), lambda bi, i: (i, src(bi)))
    const = lambda a: pl.BlockSpec(a.shape, lambda bi, i: (0,) * a.ndim)
    consts = (ga, gb, gc, dsk, wglu, wout, gf, wrh, wrl)
    return pl.pallas_call(
        functools.partial(_out_proj_kernel, a_pairs=a_pairs, b_pairs=b_pairs, nb=b),
        grid=(b + 1, nt),
        in_specs=[spec_a, spec_b, spec_c, spec_c, spec_c,
                  pl.BlockSpec((tm, d), lambda bi, i: (src(bi) * nt + i, colblk))] + [const(a) for a in consts],
        out_specs=[pl.BlockSpec((tm, 2 * d), lambda bi, i: (bi * nt + i, 0)),
                   pl.BlockSpec((1, ne, tm), lambda bi, i: (bi, 0, i))],
        out_shape=[jax.ShapeDtypeStruct(((b + 1) * s, 2 * d), F32), jax.ShapeDtypeStruct((b + 1, ne, s), F32)],
        compiler_params=_params(("arbitrary", "arbitrary")),
        name="out_proj",
    )(oa, ob, yf, yb, u2, xz, *consts)


F32_VALUE_BITS = 31
GATE_PARTS = 3
TOK_RADIX = 64


def _select_kernel(aff_ref, posm_ref, offs_ref, *, cap, ntiles):
    a = aff_ref[0]
    ne = a.shape[0]
    bits = pltpu.bitcast(a, jnp.int32)

    def refine(i, prefix):
        cand = prefix | (jnp.int32(1) << (F32_VALUE_BITS - 1 - i))
        cnt = jnp.sum(jnp.where(bits >= cand, 1.0, 0.0), axis=1, keepdims=True)
        return jnp.where(cnt >= cap, cand, prefix)

    thr = lax.fori_loop(0, F32_VALUE_BITS, refine, jnp.zeros((ne, 1), jnp.int32))
    gt = bits > thr
    eq = bits == thr
    need = cap - jnp.sum(jnp.where(gt, 1.0, 0.0), axis=1, keepdims=True)

    row = lax.broadcasted_iota(jnp.int32, (LANES, LANES), 0)
    colm = lax.broadcasted_iota(jnp.int32, (LANES, LANES), 1)
    incl = jnp.where(row <= colm, 1.0, 0.0).astype(BF16)
    wmat = jnp.concatenate([incl, jnp.ones((LANES, LANES), BF16)], axis=1)
    lane = lax.broadcasted_iota(jnp.int32, (ne, LANES), 1)
    off_gt = jnp.zeros((ne, LANES), F32)
    off_eq = jnp.zeros((ne, LANES), F32)
    offs = jnp.zeros((ne, LANES), F32)
    for j in range(ntiles):
        sl = slice(j * LANES, (j + 1) * LANES)
        g = jnp.where(gt[:, sl], 1.0, 0.0)
        q = jnp.where(eq[:, sl], 1.0, 0.0)
        r = jnp.dot(jnp.concatenate([g, q], axis=0).astype(BF16), wmat, preferred_element_type=F32)
        before_gt = off_gt + r[:ne, :LANES] - g
        before_eq = off_eq + r[ne:, :LANES] - q
        sel = gt[:, sl] | (eq[:, sl] & (before_eq < need))
        posm_ref[0, :, sl] = jnp.where(sel, before_gt + jnp.minimum(before_eq, need), -1.0)
        offs = jnp.where(lane == j, off_gt + jnp.minimum(off_eq, need), offs)
        off_gt = off_gt + r[:ne, LANES:]
        off_eq = off_eq + r[ne:, LANES:]
    offs_ref[0] = offs.astype(jnp.int32)


def _compact_kernel(offs_ref, posm_ref, tok_ref, idx_ref, gate_ref, acc_ref, *, cap, ntiles):
    b = pl.program_id(0)
    ne = posm_ref.shape[1]
    win = 2 * LANES
    acc_ref[...] = jnp.zeros_like(acc_ref)
    srow = lax.broadcasted_iota(jnp.int32, (win, 1), 0).astype(F32)

    def tile(j, carry):
        t0 = pl.multiple_of(j * LANES, LANES)
        pos = posm_ref[0, :, pl.ds(t0, LANES)]
        bases, onehots = [], []
        for e in range(ne):
            base = pl.multiple_of((offs_ref[(b * ne + e) * LANES + j] // LANES) * LANES, LANES)
            rel = pos[e:e + 1, :] - base.astype(F32)
            onehots.append(jnp.where(rel == srow, 1.0, 0.0).astype(BF16))
            bases.append(base)
        r = jnp.dot(jnp.concatenate(onehots, axis=0), tok_ref[0, pl.ds(t0, LANES), :],
                    preferred_element_type=F32)
        for e in range(ne):
            rows = pl.ds(bases[e], win)
            acc_ref[e, rows, :] = acc_ref[e, rows, :] + r[e * win:(e + 1) * win]
        return carry

    lax.fori_loop(0, ntiles, tile, 0)
    for e in range(ne):
        a = acc_ref[e, 0:cap, :]
        idx_ref[0, e] = (a[:, 0:1] * TOK_RADIX + a[:, 1:2]).astype(jnp.int32)
        gate = a[:, 2 + e:3 + e]
        for part in range(1, GATE_PARTS):
            gate = gate + a[:, 2 + part * ne + e:3 + part * ne + e]
        gate_ref[0, e] = gate


def _token_table(aff):
    b, ne, s = aff.shape
    t = jnp.arange(s, dtype=jnp.int32)
    assert s <= TOK_RADIX * 256
    cols = [jnp.broadcast_to((t // TOK_RADIX).astype(BF16)[None, :, None], (b, s, 1)),
            jnp.broadcast_to((t % TOK_RADIX).astype(BF16)[None, :, None], (b, s, 1))]
    rest = jnp.swapaxes(aff, 1, 2)
    for _ in range(GATE_PARTS):
        piece = rest.astype(BF16)
        cols.append(piece)
        rest = rest - piece.astype(F32)
    used = 2 + GATE_PARTS * ne
    assert used <= LANES
    cols.append(jnp.zeros((b, s, LANES - used), BF16))
    return jnp.concatenate(cols, axis=-1)


def _route(aff, cap):
    b, ne, s = aff.shape
    ntiles = s // LANES
    assert ntiles <= LANES and s % LANES == 0
    row_spec = pl.BlockSpec((1, ne, s), lambda bi: (bi, 0, 0))
    posm, offs = pl.pallas_call(
        functools.partial(_select_kernel, cap=cap, ntiles=ntiles),
        grid=(b,),
        in_specs=[row_spec],
        out_specs=[row_spec, pl.BlockSpec((1, ne, LANES), lambda bi: (bi, 0, 0))],
        out_shape=[jax.ShapeDtypeStruct((b, ne, s), F32), jax.ShapeDtypeStruct((b, ne, LANES), jnp.int32)],
        compiler_params=_params(("parallel",)),
        name="route_select",
    )(aff)
    slot_spec = pl.BlockSpec((1, ne, cap, 1), lambda bi, offs: (bi, 0, 0, 0))
    idx, gate = pl.pallas_call(
        functools.partial(_compact_kernel, cap=cap, ntiles=ntiles),
        grid_spec=pltpu.PrefetchScalarGridSpec(
            num_scalar_prefetch=1,
            grid=(b,),
            in_specs=[pl.BlockSpec((1, ne, s), lambda bi, offs: (bi, 0, 0)),
                      pl.BlockSpec((1, s, LANES), lambda bi, offs: (bi, 0, 0))],
            out_specs=[slot_spec, slot_spec],
            scratch_shapes=[pltpu.VMEM((ne, cap + 2 * LANES, LANES), F32)]),
        out_shape=[jax.ShapeDtypeStruct((b, ne, cap, 1), jnp.int32), jax.ShapeDtypeStruct((b, ne, cap, 1), F32)],
        compiler_params=_params(("parallel",)),
        name="route_compact",
    )(offs.reshape(-1), posm, _token_table(aff))
    return idx, gate


ROW_UNROLL = 8


def _moe_kernel(idx_ref, gate_ref, gf_ref, wg_ref, wu_ref, wd_ref, z_in_ref, z_ref,
                gbuf, obuf, hbuf, gsem, ssem, *, ne, nb, cap, seq, d, fchunk):
    del z_in_ref
    e, b = pl.program_id(0), pl.program_id(1)
    n = e * nb + b
    slot = n % 2
    other = 1 - slot

    def rows_of(step):
        e2, b2 = step // nb, step % nb
        return (b2 * ne + e2) * cap, b2 * seq

    def row_copies(step, fn):
        base, tok0 = rows_of(step)

        def body(i, carry):
            for k in range(ROW_UNROLL):
                r = i * ROW_UNROLL + k
                fn(r, idx_ref[base + r] + tok0)
            return carry

        lax.fori_loop(0, cap // ROW_UNROLL, body, 0)

    def gather_copy(sl, r, tok):
        return pltpu.make_async_copy(z_ref.at[pl.ds(tok, 1), :], gbuf.at[sl, pl.ds(r, 1), :], gsem.at[sl])

    def scatter_copy(sl, r, tok):
        return pltpu.make_async_copy(obuf.at[sl, pl.ds(r, 1), :], z_ref.at[pl.ds(tok, 1), pl.ds(d, d)], ssem)

    def wait_scatter(sl):
        pltpu.make_async_copy(obuf.at[sl], z_ref.at[pl.ds(0, cap), pl.ds(d, d)], ssem).wait()

    def wait_gather(sl):
        pltpu.make_async_copy(z_ref.at[pl.ds(0, cap), :], gbuf.at[sl], gsem.at[sl]).wait()

    total = ne * nb
    first, last = n == 0, n == total - 1

    @pl.when(first)
    def _():
        row_copies(0, lambda r, tok: gather_copy(0, r, tok).start())
        obuf[1] = jnp.zeros((cap, d), F32)

    wait_gather(slot)
    hbuf[...] = _rms(gbuf[slot, :, 0:d], gf_ref[...]).astype(BF16)
    obuf[slot] = jnp.zeros((cap, d), F32)
    nf = wg_ref.shape[2] // fchunk
    per = cap // nf

    base_p, tok0_p = rows_of(jnp.maximum(n - 1, 0))
    tok0_p = jnp.where(first, nb * seq, tok0_p)
    base_n, tok0_n = rows_of(jnp.minimum(n + 1, total - 1))
    for i in range(nf):
        fs = slice(i * fchunk, (i + 1) * fchunk)
        h = hbuf[...]
        g = jnp.dot(h, wg_ref[0, :, fs], preferred_element_type=F32)
        up = jnp.dot(h, wu_ref[0, :, fs], preferred_element_type=F32)
        hid = (jax.nn.silu(g) * up).astype(BF16)
        obuf[slot] = obuf[slot] + jnp.dot(hid, wd_ref[0, fs, :], preferred_element_type=F32)
        for r in range(i * per, (i + 1) * per):
            gather_copy(other, r, idx_ref[base_n + r] + tok0_n).start()
            scatter_copy(other, r, idx_ref[base_p + r] + tok0_p).start()
    wait_scatter(other)
    obuf[slot] = gbuf[slot, :, d:2 * d] + obuf[slot] * gate_ref[0, 0]

    @pl.when(last)
    def _():
        wait_gather(other)
        row_copies(n, lambda r, tok: scatter_copy(slot, r, tok).start())
        wait_scatter(slot)


def _moe(idx, gate, z, gf, wg, wu, wd, seq, fchunk=512):
    b, ne, cap, _ = idx.shape
    d, ff = wg.shape[1], wg.shape[2]
    fchunk = min(fchunk, ff)
    assert cap % ROW_UNROLL == 0 and ff % fchunk == 0 and cap % (ff // fchunk) == 0
    assert b >= 3 and z.shape[0] == (b + 1) * seq, "z carries one pad sequence after the real ones"
    once = pl.Buffered(1)
    grid_spec = pltpu.PrefetchScalarGridSpec(
        num_scalar_prefetch=1,
        grid=(ne, b),
        in_specs=[pl.BlockSpec((1, 1, cap, 1), lambda e, bi, idx: (bi, e, 0, 0)),
                  pl.BlockSpec((1, d), lambda e, bi, idx: (0, 0)),
                  pl.BlockSpec((1, d, ff), lambda e, bi, idx: (e, 0, 0), pipeline_mode=once),
                  pl.BlockSpec((1, d, ff), lambda e, bi, idx: (e, 0, 0), pipeline_mode=once),
                  pl.BlockSpec((1, ff, d), lambda e, bi, idx: (e, 0, 0), pipeline_mode=once),
                  pl.BlockSpec(memory_space=pl.ANY)],
        out_specs=pl.BlockSpec(memory_space=pl.ANY),
        scratch_shapes=[pltpu.VMEM((2, cap, 2 * d), F32), pltpu.VMEM((2, cap, d), F32), pltpu.VMEM((cap, d), BF16),
                        pltpu.SemaphoreType.DMA((2,)), pltpu.SemaphoreType.DMA(())],
    )
    return pl.pallas_call(
        functools.partial(_moe_kernel, ne=ne, nb=b, cap=cap, seq=seq, d=d, fchunk=fchunk),
        grid_spec=grid_spec,
        out_shape=jax.ShapeDtypeStruct(z.shape, z.dtype),
        input_output_aliases={6: 0},
        compiler_params=_params(("arbitrary", "arbitrary"), disable_bounds_checks=True),
        name="moe",
    )(idx.reshape(-1), gate, gf, wg, wu, wd, z)


def _tile2(g):
    return jnp.concatenate([g, g], axis=-1)


def kernel(x, attn_norm, w_in, q_norm_a, k_norm_a, q_norm_b, k_norm_b, rel_pos_bias, s5_a_re, s5_a_im, s5_log_dt, s5_b_re, s5_b_im, s5_c_re, s5_c_im, s5_d, w_glu, out_norm_a, out_norm_b, out_norm_c, w_out, ffn_norm, w_router, w_gate, w_up, w_down):
    b, s, d = x.shape
    depth = w_in.shape[0]
    assert b == SUBLANES, "the S5 scan keeps one sequence per sublane"
    tm = min(512, s)
    steps = min(64, s)
    cap = EC_CAPACITY * s // N_EXPERTS
    scale = HEAD_DIM ** -0.5
    nd = len(DILS)

    w_in_b, w_out_b, w_glu_b = w_in.astype(BF16), w_out.astype(BF16), w_glu.astype(BF16)
    w_gate_b, w_up_b, w_down_b = w_gate.astype(BF16), w_up.astype(BF16), w_down.astype(BF16)
    wr_t = jnp.swapaxes(w_router, 1, 2)
    wr_hi = wr_t.astype(BF16)
    wr_lo = (wr_t - wr_hi.astype(F32)).astype(BF16)

    xz, colblk = x.reshape(b * s, d), 0
    for l in range(depth):
        hg = jnp.stack([_tile2(q_norm_a[l]) * scale, _tile2(k_norm_a[l]),
                        _tile2(q_norm_b[l]) * scale, _tile2(k_norm_b[l])], axis=0)
        outs = _in_proj(xz, colblk, b, s, attn_norm[l][None], w_in_b[l], hg, tm)
        qa, ka, va = outs[0:nd], outs[nd:2 * nd], outs[2 * nd:3 * nd]
        qb, kb, vb, u2 = outs[3 * nd:]

        oa = _dilated(qa, ka, va)
        ob = _natten(qb, kb, vb, _na_bias(rel_pos_bias[l]))

        pf = _s5_params(s5_a_re[l, 0], s5_a_im[l, 0], s5_log_dt[l, 0], s5_b_re[l, 0], s5_b_im[l, 0],
                        s5_c_re[l, 0], s5_c_im[l, 0])
        pb = _s5_params(s5_a_re[l, 1], s5_a_im[l, 1], s5_log_dt[l, 1], s5_b_re[l, 1], s5_b_im[l, 1],
                        s5_c_re[l, 1], s5_c_im[l, 1])
        avec = jnp.broadcast_to(jnp.stack([pf[0], pf[1], pb[0], pb[1]])[:, None, :],
                                (4, SUBLANES, pf[0].shape[0]))
        yf, yb = _s5(u2, pf[2], pb[2], pf[3], pb[3], avec, steps)

        z, aff = _out_proj(oa, ob, yf, yb, u2, xz, colblk,
                           out_norm_a[l][None], out_norm_b[l][None], out_norm_c[l][None], s5_d[l][None],
                           w_glu_b[l], w_out_b[l], ffn_norm[l][None], wr_hi[l], wr_lo[l], tm)

        idx, gate = _route(aff[:b], cap)
        xz = _moe(idx, gate, z, ffn_norm[l][None], w_gate_b[l], w_up_b[l], w_down_b[l], s)
        colblk = 1
    return xz[:b * s, d:].reshape(b, s, d)
```

```python
import functools

import numpy as np
import jax
import jax.numpy as jnp
from jax import lax
from jax.experimental import pallas as pl
from jax.experimental.pallas import tpu as pltpu

HEAD_DIM = 64
A_HEADS = 8
B_HEADS = 4
DILATED_CONFIGS = ((128, 1), (512, 4), (2048, 16))
GRID_W = 64
NA_ROWS = 8
NA_COLS = 16
S5_GROUP = 16
S5_STATE = 64
N_EXPERTS = 16
EC_CAPACITY = 2
EPS = 1e-6
NEG_INF = -1e30

LANES = 128
SUBLANES = 8
VMEM_LIMIT = 56 * 1024 * 1024

HALF = 64
QBLK = 2 * HALF
KBLK = 4 * HALF
DIL_CHUNK = 2048
DIL_GROUP = 4
DILS = tuple(d for _, d in DILATED_CONFIGS)
assert all(w == 2 * HALF * d for w, d in DILATED_CONFIGS) and DILS[0] == 1

F32 = jnp.float32
BF16 = jnp.bfloat16

_NT = (((1,), (1,)), ((), ()))


def _params(sem, vmem=VMEM_LIMIT, **kw):
    return pltpu.CompilerParams(dimension_semantics=sem, vmem_limit_bytes=vmem, **kw)


def _lo_lanes():
    return lax.broadcasted_iota(jnp.int32, (1, LANES), 1) < HEAD_DIM


def _stack_heads(q2, lo):
    zero = jnp.zeros_like(q2)
    return jnp.concatenate([jnp.where(lo, q2, zero), jnp.where(lo, zero, q2)], axis=0)


def _rms(t, gain):
    return (t * lax.rsqrt(jnp.mean(t * t, axis=-1, keepdims=True) + EPS)) * gain


def _in_proj_kernel(x_ref, g_ref, w_ref, hg_ref, *rest, a_pairs, b_pairs, tm, nsplit):
    nd = len(DILS)
    qa_refs, ka_refs, va_refs = rest[0:nd], rest[nd:2 * nd], rest[2 * nd:3 * nd]
    qb_ref, kb_ref, vb_ref, u_ref, scr = rest[3 * nd:]
    lo = _lo_lanes()

    def head_norm(c, gain_row):
        if gain_row is None:
            return c
        sq = c * c
        s_lo = jnp.sum(jnp.where(lo, sq, 0.0), axis=-1, keepdims=True)
        s_hi = jnp.sum(jnp.where(lo, 0.0, sq), axis=-1, keepdims=True)
        r = jnp.where(lo, lax.rsqrt(s_lo / HEAD_DIM + EPS), lax.rsqrt(s_hi / HEAD_DIM + EPS))
        return (c * r) * hg_ref[gain_row:gain_row + 1, :]

    sub = tm // nsplit
    for st in range(nsplit):
        rows = slice(st * sub, (st + 1) * sub)
        h = _rms(x_ref[rows, :], g_ref[...])
        proj = jnp.dot(h.astype(BF16), w_ref[...], preferred_element_type=F32)
        col = 0
        nscr = 0
        for refs, gain_row in ((qa_refs, 0), (ka_refs, 1), (va_refs, None)):
            for p in range(a_pairs):
                c = head_norm(proj[:, col:col + LANES], gain_row)
                refs[0][0, p, rows, :] = c.astype(BF16)
                scr[nscr, rows, :] = c
                for ref, dil in zip(refs[1:], DILS[1:]):
                    vrows = slice(st * sub // dil, (st + 1) * sub // dil)
                    for r in range(dil):
                        ref[0, p, vrows, r * LANES:(r + 1) * LANES] = (
                            scr[nscr, pl.ds(st * sub + r, sub // dil, stride=dil), :].astype(BF16))
                nscr += 1
                col += LANES
        for ref, gain_row in ((qb_ref, 2), (kb_ref, 3), (vb_ref, None)):
            for p in range(b_pairs):
                ref[0, p, rows, :] = head_norm(proj[:, col:col + LANES], gain_row).astype(BF16)
                col += LANES
        u_ref[rows, :] = proj[:, col:]


def _in_proj(xz, colblk, b, s, g, w, hg, tm, nsplit=1):
    assert tm % (nsplit * 2 * SUBLANES * DILS[-1]) == 0
    d = g.shape[1]
    nt = s // tm
    a_pairs, b_pairs = A_HEADS // 2, B_HEADS // 2
    cw = w.shape[1] - 3 * (a_pairs + b_pairs) * LANES
    shapes_a = [jax.ShapeDtypeStruct((b, a_pairs, s // dil, dil * LANES), BF16) for dil in DILS]
    specs_a = [pl.BlockSpec((1, a_pairs, tm // dil, dil * LANES), lambda bi, i: (bi, 0, i, 0)) for dil in DILS]
    shape_b = jax.ShapeDtypeStruct((b, b_pairs, s, LANES), BF16)
    spec_b = pl.BlockSpec((1, b_pairs, tm, LANES), lambda bi, i: (bi, 0, i, 0))
    return pl.pallas_call(
        functools.partial(_in_proj_kernel, a_pairs=a_pairs, b_pairs=b_pairs, tm=tm, nsplit=nsplit),
        grid=(b, nt),
        in_specs=[pl.BlockSpec((tm, d), lambda bi, i: (bi * nt + i, colblk)),
                  pl.BlockSpec((1, d), lambda bi, i: (0, 0)),
                  pl.BlockSpec(w.shape, lambda bi, i: (0, 0)),
                  pl.BlockSpec(hg.shape, lambda bi, i: (0, 0))],
        out_specs=specs_a * 3 + [spec_b] * 3 + [pl.BlockSpec((tm, cw), lambda bi, i: (i, bi))],
        out_shape=shapes_a * 3 + [shape_b] * 3 + [jax.ShapeDtypeStruct((s, b * cw), F32)],
        scratch_shapes=[pltpu.VMEM((3 * a_pairs, tm, LANES), F32)],
        compiler_params=_params(("parallel", "parallel")),
        name="in_proj",
    )(xz, g, w, hg)


def _dilated_kernel(*refs, chunk):
    nd = len(DILS)
    o_ref, acc_ref, lse_ref = refs[8 * nd:]
    c = pl.program_id(2)
    nc = pl.num_programs(2)
    lo = _lo_lanes()
    kcol = lax.broadcasted_iota(jnp.int32, (1, KBLK), 1)
    edge_first = jnp.where((kcol < HALF) & (c == 0), NEG_INF, 0.0)
    edge_last = jnp.where((kcol >= KBLK - HALF) & (c == nc - 1), NEG_INF, 0.0)
    for bi, dil in enumerate(DILS):
        q_ref, kp_ref, k_ref, kn_ref, vp_ref, v_ref, vn_ref, bias_ref = refs[8 * bi:8 * bi + 8]
        nblk = chunk // dil // QBLK
        bias = bias_ref[0]
        jobs = [(r, m) for r in range(dil) for m in range(nblk)]
        bands = {}
        for g0 in range(0, len(jobs), DIL_GROUP):
            group = jobs[g0:g0 + DIL_GROUP]
            scores, values = [], []
            for r, m in group:
                cs = slice(r * LANES, (r + 1) * LANES)
                if r not in bands:
                    bands[r] = tuple(jnp.concatenate([a[0, 0, :, cs], b_[0, 0, :, cs], c_[0, 0, :, cs]], axis=0)
                                     for a, b_, c_ in ((kp_ref, k_ref, kn_ref), (vp_ref, v_ref, vn_ref)))
                kc, vc = bands[r]
                qs = _stack_heads(q_ref[0, 0, m * QBLK:(m + 1) * QBLK, cs], lo)
                s = lax.dot_general(qs, kc[m * QBLK:m * QBLK + KBLK], _NT, preferred_element_type=F32) + bias
                if m == 0:
                    s = s + edge_first
                if m == nblk - 1:
                    s = s + edge_last
                scores.append(s)
                values.append(vc[m * QBLK:m * QBLK + KBLK])
            soft = []
            for s in scores:
                mx = jnp.max(s, axis=-1, keepdims=True)
                p = jnp.exp(s - mx)
                soft.append((p.astype(BF16), mx, jnp.sum(p, axis=-1, keepdims=True)))
            for (r, m), (p, mx, den), vv in zip(group, soft, values):
                o = jnp.dot(p, vv, preferred_element_type=F32) / den
                lse = mx + jnp.log(den)
                o2 = jnp.where(lo, o[:QBLK], o[QBLK:])
                l2 = jnp.where(lo, lse[:QBLK], lse[QBLK:])
                tok = pl.ds(m * QBLK * dil + r, QBLK, stride=dil) if dil > 1 else pl.ds(m * QBLK, QBLK)
                if bi == 0:
                    acc_ref[tok, :] = o2
                    lse_ref[tok, :] = l2
                else:
                    a_old, l_old = acc_ref[tok, :], lse_ref[tok, :]
                    m2 = jnp.maximum(l_old, l2)
                    e_old, e_new = jnp.exp(l_old - m2), jnp.exp(l2 - m2)
                    tot = e_old + e_new
                    acc_ref[tok, :] = (a_old * e_old + o2 * e_new) / tot
                    if bi < nd - 1:
                        lse_ref[tok, :] = m2 + jnp.log(tot)
    o_ref[0, 0] = acc_ref[...].astype(BF16)


def _alibi_slopes(n):
    return np.array([2.0 ** (-8.0 * (i + 1) / n) for i in range(n)], dtype=np.float32)


def _dilated_bias(dil):
    rel = (np.arange(KBLK)[None, :] - HALF) - np.arange(QBLK)[:, None]
    valid = np.abs(rel) <= HALF
    dist = (dil * np.abs(rel)).astype(np.float32)
    slopes = _alibi_slopes(A_HEADS)
    per_head = np.where(valid[None], -(slopes[:, None, None] * dist[None]), np.float32(NEG_INF)).astype(np.float32)
    return per_head.reshape(A_HEADS // 2, 2 * QBLK, KBLK)


def _dilated(q_views, k_views, v_views):
    b, hp, s, _ = q_views[0].shape
    chunk = min(DIL_CHUNK, s)
    assert s % chunk == 0 and chunk % (QBLK * DILS[-1]) == 0
    args, specs = [], []
    for q, k, v, dil in zip(q_views, k_views, v_views, DILS):
        rows, width = chunk // dil, dil * LANES
        hb = rows // HALF
        last = s // dil // HALF - 1
        main = pl.BlockSpec((1, 1, rows, width), lambda bi, h, c: (bi, h, c, 0))
        prev = pl.BlockSpec((1, 1, HALF, width), lambda bi, h, c, hb=hb: (bi, h, jnp.maximum(c * hb - 1, 0), 0))
        nxt = pl.BlockSpec((1, 1, HALF, width),
                           lambda bi, h, c, hb=hb, last=last: (bi, h, jnp.minimum((c + 1) * hb, last), 0))
        args += [q, k, k, k, v, v, v, jnp.asarray(_dilated_bias(dil))]
        specs += [main, prev, main, nxt, prev, main, nxt,
                  pl.BlockSpec((1, 2 * QBLK, KBLK), lambda bi, h, c: (h, 0, 0))]
    return pl.pallas_call(
        functools.partial(_dilated_kernel, chunk=chunk),
        grid=(b, hp, s // chunk),
        in_specs=specs,
        out_specs=pl.BlockSpec((1, 1, chunk, LANES), lambda bi, h, c: (bi, h, c, 0)),
        out_shape=jax.ShapeDtypeStruct((b, hp, s, LANES), BF16),
        scratch_shapes=[pltpu.VMEM((chunk, LANES), F32), pltpu.VMEM((chunk, LANES), F32)],
        compiler_params=_params(("parallel", "parallel", "parallel")),
        name="dilated",
    )(*args)


def _na_kernel(q_ref, k_ref, v_ref, bias_ref, o_ref, *, rb, nrows):
    i = pl.program_id(2)
    lo = _lo_lanes()
    kspan = NA_ROWS * GRID_W
    scores, values = [], []
    for j in range(rb):
        r = i * rb + j
        rs = jnp.clip(r - NA_ROWS // 2, 0, nrows - NA_ROWS)
        start = pl.multiple_of(rs * GRID_W, GRID_W)
        kk = k_ref[0, 0, pl.ds(start, kspan), :]
        values.append(v_ref[0, 0, pl.ds(start, kspan), :])
        qs = _stack_heads(q_ref[0, 0, j * GRID_W:(j + 1) * GRID_W, :], lo)
        scores.append(lax.dot_general(qs, kk, _NT, preferred_element_type=F32) + bias_ref[0, r - rs])
    probs, dens = [], []
    for s in scores:
        p = jnp.exp(s - jnp.max(s, axis=-1, keepdims=True))
        dens.append(jnp.sum(p, axis=-1, keepdims=True))
        probs.append(p.astype(BF16))
    for j in range(rb):
        o = jnp.dot(probs[j], values[j], preferred_element_type=F32) / dens[j]
        o_ref[0, 0, j * GRID_W:(j + 1) * GRID_W, :] = jnp.where(lo, o[:GRID_W], o[GRID_W:]).astype(BF16)


def _na_bias(rpb):
    t = np.arange(NA_ROWS, dtype=np.int32)[:, None, None, None]
    krow = np.arange(NA_ROWS, dtype=np.int32)[None, None, :, None]
    qc = np.arange(GRID_W, dtype=np.int32)[None, :, None, None]
    kc = np.arange(GRID_W, dtype=np.int32)[None, None, None, :]
    dr = np.broadcast_to(krow - t + NA_ROWS - 1, (NA_ROWS, GRID_W, NA_ROWS, GRID_W))
    dc = np.broadcast_to(np.clip(kc - qc + NA_COLS - 1, 0, 2 * NA_COLS - 2), dr.shape)
    cs = np.clip(qc - NA_COLS // 2, 0, GRID_W - NA_COLS)
    valid = np.broadcast_to((kc >= cs) & (kc < cs + NA_COLS), dr.shape)
    row_sel = np.equal(dr[:, 0, :, 0][..., None], np.arange(2 * NA_ROWS - 1)).astype(np.float32)
    col_sel = np.equal(dc[0, :, 0, :][..., None], np.arange(2 * NA_COLS - 1)).astype(np.float32)
    tab = jnp.einsum('hab,qcb->haqc', rpb.astype(F32), col_sel, precision=lax.Precision.HIGHEST)
    tab = jnp.einsum('haqc,tka->htqkc', tab, row_sel, precision=lax.Precision.HIGHEST)
    tab = jnp.where(valid[None], tab, NEG_INF).reshape(B_HEADS // 2, 2, NA_ROWS, GRID_W, NA_ROWS * GRID_W)
    return tab.transpose(0, 2, 1, 3, 4).reshape(B_HEADS // 2, NA_ROWS, 2 * GRID_W, NA_ROWS * GRID_W)


def _natten(q, k, v, bias, rb=16):
    b, hp, s, _ = q.shape
    nrows = s // GRID_W
    assert nrows >= NA_ROWS and nrows % rb == 0
    full = pl.BlockSpec((1, 1, s, LANES), lambda bi, h, i: (bi, h, 0, 0))
    tile = pl.BlockSpec((1, 1, rb * GRID_W, LANES), lambda bi, h, i: (bi, h, i, 0))
    return pl.pallas_call(
        functools.partial(_na_kernel, rb=rb, nrows=nrows),
        grid=(b, hp, nrows // rb),
        in_specs=[tile, full, full,
                  pl.BlockSpec((1,) + bias.shape[1:], lambda bi, h, i: (h, 0, 0, 0))],
        out_specs=tile,
        out_shape=jax.ShapeDtypeStruct(q.shape, BF16),
        compiler_params=_params(("parallel", "parallel", "arbitrary")),
        name="natten",
    )(q, k, v, bias)


def _s5_kernel(uf_ref, ub_ref, bf_ref, bb_ref, cf_ref, cb_ref, a_ref, yf_ref, yb_ref,
               xf_ref, xb_ref, io_ref, st_ref, *, steps, nstate, cw):
    @pl.when(pl.program_id(0) == 0)
    def _():
        st_ref[...] = jnp.zeros_like(st_ref)

    nlt = cw // LANES

    def time_major(u_ref):
        for bi in range(SUBLANES):
            for j in range(nlt):
                col = bi * cw + j * LANES
                io_ref[j, pl.ds(bi, steps, stride=SUBLANES), :] = u_ref[:, col:col + LANES]
        return jnp.concatenate([io_ref[j] for j in range(nlt)], axis=-1).astype(BF16)

    xf_ref[...] = jnp.dot(time_major(uf_ref), bf_ref[...], preferred_element_type=F32)
    xb_ref[...] = jnp.dot(time_major(ub_ref), bb_ref[...], preferred_element_type=F32)
    re, im = slice(0, nstate), slice(nstate, 2 * nstate)

    def step(t, carry):
        xrf, xif, xrb, xib = carry
        rf = pl.ds(pl.multiple_of(t * SUBLANES, SUBLANES), SUBLANES)
        rb = pl.ds(pl.multiple_of((steps - 1 - t) * SUBLANES, SUBLANES), SUBLANES)
        nrf = a_ref[0] * xrf - a_ref[1] * xif + xf_ref[rf, re]
        nif = a_ref[0] * xif + a_ref[1] * xrf + xf_ref[rf, im]
        nrb = a_ref[2] * xrb - a_ref[3] * xib + xb_ref[rb, re]
        nib = a_ref[2] * xib + a_ref[3] * xrb + xb_ref[rb, im]
        xf_ref[rf, re] = nrf
        xf_ref[rf, im] = nif
        xb_ref[rb, re] = nrb
        xb_ref[rb, im] = nib
        return nrf, nif, nrb, nib

    carry = lax.fori_loop(0, steps, step, (st_ref[0], st_ref[1], st_ref[2], st_ref[3]))
    for n, val in enumerate(carry):
        st_ref[n] = val

    def batch_major(x_ref, c_ref, y_ref):
        y = jnp.dot(x_ref[...].astype(BF16), c_ref[...], preferred_element_type=F32)
        for j in range(nlt):
            io_ref[j] = y[:, j * LANES:(j + 1) * LANES]
        for bi in range(SUBLANES):
            for j in range(nlt):
                col = bi * cw + j * LANES
                y_ref[:, col:col + LANES] = io_ref[j, pl.ds(bi, steps, stride=SUBLANES), :]

    batch_major(xf_ref, cf_ref, yf_ref)
    batch_major(xb_ref, cb_ref, yb_ref)


def _s5_params(a_re, a_im, log_dt, b_re, b_im, c_re, c_im):
    groups = a_re.shape[0]
    dt = jnp.exp(log_dt.astype(F32))[:, None]
    a = jnp.minimum(a_re.astype(F32), -1e-4)
    w = a_im.astype(F32)
    mag = jnp.exp(a * dt)
    abar_r, abar_i = mag * jnp.cos(w * dt), mag * jnp.sin(w * dt)
    den = a * a + w * w
    zr = abar_r - 1.0
    gr = (zr * a + abar_i * w) / den
    gi = (abar_i * a - zr * w) / den
    br, bi = b_re.astype(F32), b_im.astype(F32)
    bbar_r = gr[..., None] * br - gi[..., None] * bi
    bbar_i = gr[..., None] * bi + gi[..., None] * br
    eye = jnp.eye(groups, dtype=F32)
    n = groups * S5_STATE

    def in_mat(t):
        return jnp.einsum('gpc,gh->gchp', t, eye).reshape(groups * S5_GROUP, n)

    def out_mat(t):
        return jnp.einsum('gcp,gh->gphc', t, eye).reshape(n, groups * S5_GROUP)

    bmat = jnp.concatenate([in_mat(bbar_r), in_mat(bbar_i)], axis=1).astype(BF16)
    cmat = jnp.concatenate([out_mat(c_re.astype(F32)), out_mat(-c_im.astype(F32))], axis=0).astype(BF16)
    return abar_r.reshape(n), abar_i.reshape(n), bmat, cmat


def _s5(u2, bmat_f, bmat_b, cmat_f, cmat_b, avec, steps):
    s, width = u2.shape
    cw = width // SUBLANES
    nstate = avec.shape[-1]
    n = s // steps
    tr = steps * SUBLANES
    const = lambda shape: pl.BlockSpec(shape, lambda i: (0,) * len(shape))
    return pl.pallas_call(
        functools.partial(_s5_kernel, steps=steps, nstate=nstate, cw=cw),
        grid=(n,),
        in_specs=[pl.BlockSpec((steps, width), lambda i: (i, 0)),
                  pl.BlockSpec((steps, width), lambda i: (n - 1 - i, 0)),
                  const(bmat_f.shape), const(bmat_b.shape), const(cmat_f.shape), const(cmat_b.shape),
                  const(avec.shape)],
        out_specs=[pl.BlockSpec((steps, width), lambda i: (i, 0)),
                   pl.BlockSpec((steps, width), lambda i: (n - 1 - i, 0))],
        out_shape=[jax.ShapeDtypeStruct(u2.shape, F32), jax.ShapeDtypeStruct(u2.shape, F32)],
        scratch_shapes=[pltpu.VMEM((tr, 2 * nstate), F32), pltpu.VMEM((tr, 2 * nstate), F32),
                        pltpu.VMEM((cw // LANES, tr, LANES), F32), pltpu.VMEM((4, SUBLANES, nstate), F32)],
        compiler_params=_params(("arbitrary",)),
        name="s5_scan",
    )(u2, u2, bmat_f, bmat_b, cmat_f, cmat_b, avec)


def _out_proj_kernel(oa_ref, ob_ref, yf_ref, yb_ref, u_ref, x_ref,
                     ga_ref, gb_ref, gc_ref, dsk_ref, wglu_ref, wout_ref, gf_ref, wrh_ref, wrl_ref,
                     z_ref, aff_ref, *, a_pairs, b_pairs):
    oa = jnp.concatenate([oa_ref[0, p].astype(F32) for p in range(a_pairs)], axis=-1)
    ob = jnp.concatenate([ob_ref[0, p].astype(F32) for p in range(b_pairs)], axis=-1)
    y = (yf_ref[...] + yb_ref[...]) + dsk_ref[...] * u_ref[...]
    g = jax.nn.gelu(y)
    oc = g * jax.nn.sigmoid(jnp.dot(g.astype(BF16), wglu_ref[...], preferred_element_type=F32))
    mix = jnp.concatenate([_rms(oa, ga_ref[...]), _rms(ob, gb_ref[...]), _rms(oc, gc_ref[...])], axis=-1)
    x1 = x_ref[...] + jnp.dot(mix.astype(BF16), wout_ref[...], preferred_element_type=F32)
    d = x1.shape[1]
    z_ref[:, 0:d] = x1
    z_ref[:, d:2 * d] = x1
    h = _rms(x1, gf_ref[...])
    h_hi = h.astype(BF16)
    h_lo = (h - h_hi.astype(F32)).astype(BF16)
    logits = (lax.dot_general(wrh_ref[...], h_hi, _NT, preferred_element_type=F32)
              + lax.dot_general(wrh_ref[...], h_lo, _NT, preferred_element_type=F32)
              + lax.dot_general(wrl_ref[...], h_hi, _NT, preferred_element_type=F32))
    e = jnp.exp(logits - jnp.max(logits, axis=0, keepdims=True))
    aff_ref[0] = e / jnp.sum(e, axis=0, keepdims=True)


def _out_proj(oa, ob, yf, yb, u2, xz, colblk, ga, gb, gc, dsk, wglu, wout, gf, wrh, wrl, tm):
    b, a_pairs, s, _ = oa.shape
    b_pairs = ob.shape[1]
    d = wout.shape[1]
    nt = s // tm
    cw = u2.shape[1] // b
    ne = wrh.shape[0]
    src = lambda bi: jnp.minimum(bi, b - 1)
    spec_a = pl.BlockSpec((1, a_pairs, tm, LANES), lambda bi, i: (src(bi), 0, i, 0))
    spec_b = pl.BlockSpec((1, b_pairs, tm, LANES), lambda bi, i: (src(bi), 0, i, 0))
    spec_c = pl.BlockSpec((tm, cw), lambda bi, i: (i, src(bi)))
    const = lambda a: pl.BlockSpec(a.shape, lambda bi, i: (0,) * a.ndim)
    consts = (ga, gb, gc, dsk, wglu, wout, gf, wrh, wrl)
    return pl.pallas_call(
        functools.partial(_out_proj_kernel, a_pairs=a_pairs, b_pairs=b_pairs),
        grid=(b + 1, nt),
        in_specs=[spec_a, spec_b, spec_c, spec_c, spec_c,
                  pl.BlockSpec((tm, d), lambda bi, i: (src(bi) * nt + i, colblk))] + [const(a) for a in consts],
        out_specs=[pl.BlockSpec((tm, 2 * d), lambda bi, i: (bi * nt + i, 0)),
                   pl.BlockSpec((1, ne, tm), lambda bi, i: (bi, 0, i))],
        out_shape=[jax.ShapeDtypeStruct(((b + 1) * s, 2 * d), F32), jax.ShapeDtypeStruct((b + 1, ne, s), F32)],
        compiler_params=_params(("arbitrary", "arbitrary")),
        name="out_proj",
    )(oa, ob, yf, yb, u2, xz, *consts)


F32_VALUE_BITS = 31
GATE_PARTS = 3
TOK_RADIX = 64


def _select_kernel(aff_ref, posm_ref, offs_ref, *, cap, ntiles):
    a = aff_ref[0]
    ne = a.shape[0]
    bits = pltpu.bitcast(a, jnp.int32)

    def refine(i, prefix):
        cand = prefix | (jnp.int32(1) << (F32_VALUE_BITS - 1 - i))
        cnt = jnp.sum(jnp.where(bits >= cand, 1.0, 0.0), axis=1, keepdims=True)
        return jnp.where(cnt >= cap, cand, prefix)

    thr = lax.fori_loop(0, F32_VALUE_BITS, refine, jnp.zeros((ne, 1), jnp.int32))
    gt = bits > thr
    eq = bits == thr
    need = cap - jnp.sum(jnp.where(gt, 1.0, 0.0), axis=1, keepdims=True)

    row = lax.broadcasted_iota(jnp.int32, (LANES, LANES), 0)
    colm = lax.broadcasted_iota(jnp.int32, (LANES, LANES), 1)
    incl = jnp.where(row <= colm, 1.0, 0.0).astype(BF16)
    wmat = jnp.concatenate([incl, jnp.ones((LANES, LANES), BF16)], axis=1)
    lane = lax.broadcasted_iota(jnp.int32, (ne, LANES), 1)
    off_gt = jnp.zeros((ne, LANES), F32)
    off_eq = jnp.zeros((ne, LANES), F32)
    offs = jnp.zeros((ne, LANES), F32)
    for j in range(ntiles):
        sl = slice(j * LANES, (j + 1) * LANES)
        g = jnp.where(gt[:, sl], 1.0, 0.0)
        q = jnp.where(eq[:, sl], 1.0, 0.0)
        r = jnp.dot(jnp.concatenate([g, q], axis=0).astype(BF16), wmat, preferred_element_type=F32)
        before_gt = off_gt + r[:ne, :LANES] - g
        before_eq = off_eq + r[ne:, :LANES] - q
        sel = gt[:, sl] | (eq[:, sl] & (before_eq < need))
        posm_ref[0, :, sl] = jnp.where(sel, before_gt + jnp.minimum(before_eq, need), -1.0)
        offs = jnp.where(lane == j, off_gt + jnp.minimum(off_eq, need), offs)
        off_gt = off_gt + r[:ne, LANES:]
        off_eq = off_eq + r[ne:, LANES:]
    offs_ref[0] = offs.astype(jnp.int32)


def _compact_kernel(offs_ref, posm_ref, tok_ref, idx_ref, gate_ref, acc_ref, *, cap, ntiles):
    b = pl.program_id(0)
    ne = posm_ref.shape[1]
    win = 2 * LANES
    acc_ref[...] = jnp.zeros_like(acc_ref)
    srow = lax.broadcasted_iota(jnp.int32, (win, 1), 0).astype(F32)

    def tile(j, carry):
        t0 = pl.multiple_of(j * LANES, LANES)
        pos = posm_ref[0, :, pl.ds(t0, LANES)]
        bases, onehots = [], []
        for e in range(ne):
            base = pl.multiple_of((offs_ref[(b * ne + e) * LANES + j] // LANES) * LANES, LANES)
            rel = pos[e:e + 1, :] - base.astype(F32)
            onehots.append(jnp.where(rel == srow, 1.0, 0.0).astype(BF16))
            bases.append(base)
        r = jnp.dot(jnp.concatenate(onehots, axis=0), tok_ref[0, pl.ds(t0, LANES), :],
                    preferred_element_type=F32)
        for e in range(ne):
            rows = pl.ds(bases[e], win)
            acc_ref[e, rows, :] = acc_ref[e, rows, :] + r[e * win:(e + 1) * win]
        return carry

    lax.fori_loop(0, ntiles, tile, 0)
    for e in range(ne):
        a = acc_ref[e, 0:cap, :]
        idx_ref[0, e] = (a[:, 0:1] * TOK_RADIX + a[:, 1:2]).astype(jnp.int32)
        gate = a[:, 2 + e:3 + e]
        for part in range(1, GATE_PARTS):
            gate = gate + a[:, 2 + part * ne + e:3 + part * ne + e]
        gate_ref[0, e] = gate


def _token_table(aff):
    b, ne, s = aff.shape
    t = jnp.arange(s, dtype=jnp.int32)
    assert s <= TOK_RADIX * 256
    cols = [jnp.broadcast_to((t // TOK_RADIX).astype(BF16)[None, :, None], (b, s, 1)),
            jnp.broadcast_to((t % TOK_RADIX).astype(BF16)[None, :, None], (b, s, 1))]
    rest = jnp.swapaxes(aff, 1, 2)
    for _ in range(GATE_PARTS):
        piece = rest.astype(BF16)
        cols.append(piece)
        rest = rest - piece.astype(F32)
    used = 2 + GATE_PARTS * ne
    assert used <= LANES
    cols.append(jnp.zeros((b, s, LANES - used), BF16))
    return jnp.concatenate(cols, axis=-1)


def _route(aff, cap):
    b, ne, s = aff.shape
    ntiles = s // LANES
    assert ntiles <= LANES and s % LANES == 0
    row_spec = pl.BlockSpec((1, ne, s), lambda bi: (bi, 0, 0))
    posm, offs = pl.pallas_call(
        functools.partial(_select_kernel, cap=cap, ntiles=ntiles),
        grid=(b,),
        in_specs=[row_spec],
        out_specs=[row_spec, pl.BlockSpec((1, ne, LANES), lambda bi: (bi, 0, 0))],
        out_shape=[jax.ShapeDtypeStruct((b, ne, s), F32), jax.ShapeDtypeStruct((b, ne, LANES), jnp.int32)],
        compiler_params=_params(("parallel",)),
        name="route_select",
    )(aff)
    slot_spec = pl.BlockSpec((1, ne, cap, 1), lambda bi, offs: (bi, 0, 0, 0))
    idx, gate = pl.pallas_call(
        functools.partial(_compact_kernel, cap=cap, ntiles=ntiles),
        grid_spec=pltpu.PrefetchScalarGridSpec(
            num_scalar_prefetch=1,
            grid=(b,),
            in_specs=[pl.BlockSpec((1, ne, s), lambda bi, offs: (bi, 0, 0)),
                      pl.BlockSpec((1, s, LANES), lambda bi, offs: (bi, 0, 0))],
            out_specs=[slot_spec, slot_spec],
            scratch_shapes=[pltpu.VMEM((ne, cap + 2 * LANES, LANES), F32)]),
        out_shape=[jax.ShapeDtypeStruct((b, ne, cap, 1), jnp.int32), jax.ShapeDtypeStruct((b, ne, cap, 1), F32)],
        compiler_params=_params(("parallel",)),
        name="route_compact",
    )(offs.reshape(-1), posm, _token_table(aff))
    return idx, gate


ROW_UNROLL = 8


def _moe_kernel(idx_ref, gate_ref, gf_ref, wg_ref, wu_ref, wd_ref, z_in_ref, z_ref,
                gbuf, obuf, hbuf, gsem, ssem, *, ne, nb, cap, seq, d, fchunk):
    del z_in_ref
    e, b = pl.program_id(0), pl.program_id(1)
    n = e * nb + b
    slot = n % 2
    other = 1 - slot

    def rows_of(step):
        e2, b2 = step // nb, step % nb
        return (b2 * ne + e2) * cap, b2 * seq

    def row_copies(step, fn):
        base, tok0 = rows_of(step)

        def body(i, carry):
            for k in range(ROW_UNROLL):
                r = i * ROW_UNROLL + k
                fn(r, idx_ref[base + r] + tok0)
            return carry

        lax.fori_loop(0, cap // ROW_UNROLL, body, 0)

    def gather_copy(sl, r, tok):
        return pltpu.make_async_copy(z_ref.at[pl.ds(tok, 1), :], gbuf.at[sl, pl.ds(r, 1), :], gsem.at[sl])

    def scatter_copy(sl, r, tok):
        return pltpu.make_async_copy(obuf.at[sl, pl.ds(r, 1), :], z_ref.at[pl.ds(tok, 1), pl.ds(d, d)], ssem)

    def wait_scatter(sl):
        pltpu.make_async_copy(obuf.at[sl], z_ref.at[pl.ds(0, cap), pl.ds(d, d)], ssem).wait()

    def wait_gather(sl):
        pltpu.make_async_copy(z_ref.at[pl.ds(0, cap), :], gbuf.at[sl], gsem.at[sl]).wait()

    total = ne * nb
    first, last = n == 0, n == total - 1

    @pl.when(first)
    def _():
        row_copies(0, lambda r, tok: gather_copy(0, r, tok).start())
        obuf[1] = jnp.zeros((cap, d), F32)

    wait_gather(slot)
    hbuf[...] = _rms(gbuf[slot, :, 0:d], gf_ref[...]).astype(BF16)
    obuf[slot] = jnp.zeros((cap, d), F32)
    nf = wg_ref.shape[2] // fchunk
    per = cap // nf

    base_p, tok0_p = rows_of(jnp.maximum(n - 1, 0))
    tok0_p = jnp.where(first, nb * seq, tok0_p)
    base_n, tok0_n = rows_of(jnp.minimum(n + 1, total - 1))
    for i in range(nf):
        fs = slice(i * fchunk, (i + 1) * fchunk)
        h = hbuf[...]
        g = jnp.dot(h, wg_ref[0, :, fs], preferred_element_type=F32)
        up = jnp.dot(h, wu_ref[0, :, fs], preferred_element_type=F32)
        hid = (jax.nn.silu(g) * up).astype(BF16)
        obuf[slot] = obuf[slot] + jnp.dot(hid, wd_ref[0, fs, :], preferred_element_type=F32)
        for r in range(i * per, (i + 1) * per):
            gather_copy(other, r, idx_ref[base_n + r] + tok0_n).start()
            scatter_copy(other, r, idx_ref[base_p + r] + tok0_p).start()
    wait_scatter(other)
    obuf[slot] = gbuf[slot, :, d:2 * d] + obuf[slot] * gate_ref[0, 0]

    @pl.when(last)
    def _():
        wait_gather(other)
        row_copies(n, lambda r, tok: scatter_copy(slot, r, tok).start())
        wait_scatter(slot)


def _moe(idx, gate, z, gf, wg, wu, wd, layer, seq, fchunk=512):
    b, ne, cap, _ = idx.shape
    d, ff = wg.shape[2], wg.shape[3]
    fchunk = min(fchunk, ff)
    assert cap % ROW_UNROLL == 0 and ff % fchunk == 0 and cap % (ff // fchunk) == 0
    assert b >= 3 and z.shape[0] == (b + 1) * seq, "z carries one pad sequence after the real ones"
    once = pl.Buffered(1)
    grid_spec = pltpu.PrefetchScalarGridSpec(
        num_scalar_prefetch=1,
        grid=(ne, b),
        in_specs=[pl.BlockSpec((1, 1, cap, 1), lambda e, bi, idx: (bi, e, 0, 0)),
                  pl.BlockSpec((1, d), lambda e, bi, idx: (0, 0)),
                  pl.BlockSpec((None, 1, d, ff), lambda e, bi, idx: (layer, e, 0, 0), pipeline_mode=once),
                  pl.BlockSpec((None, 1, d, ff), lambda e, bi, idx: (layer, e, 0, 0), pipeline_mode=once),
                  pl.BlockSpec((None, 1, ff, d), lambda e, bi, idx: (layer, e, 0, 0), pipeline_mode=once),
                  pl.BlockSpec(memory_space=pl.ANY)],
        out_specs=pl.BlockSpec(memory_space=pl.ANY),
        scratch_shapes=[pltpu.VMEM((2, cap, 2 * d), F32), pltpu.VMEM((2, cap, d), F32), pltpu.VMEM((cap, d), BF16),
                        pltpu.SemaphoreType.DMA((2,)), pltpu.SemaphoreType.DMA(())],
    )
    return pl.pallas_call(
        functools.partial(_moe_kernel, ne=ne, nb=b, cap=cap, seq=seq, d=d, fchunk=fchunk),
        grid_spec=grid_spec,
        out_shape=jax.ShapeDtypeStruct(z.shape, z.dtype),
        input_output_aliases={6: 0},
        compiler_params=_params(("arbitrary", "arbitrary"), disable_bounds_checks=True),
        name="moe",
    )(idx.reshape(-1), gate, gf, wg, wu, wd, z)


def _tile2(g):
    return jnp.concatenate([g, g], axis=-1)


def kernel(x, attn_norm, w_in, q_norm_a, k_norm_a, q_norm_b, k_norm_b, rel_pos_bias, s5_a_re, s5_a_im, s5_log_dt, s5_b_re, s5_b_im, s5_c_re, s5_c_im, s5_d, w_glu, out_norm_a, out_norm_b, out_norm_c, w_out, ffn_norm, w_router, w_gate, w_up, w_down):
    b, s, d = x.shape
    depth = w_in.shape[0]
    assert b == SUBLANES, "the S5 scan keeps one sequence per sublane"
    tm = min(512, s)
    steps = min(64, s)
    cap = EC_CAPACITY * s // N_EXPERTS
    scale = HEAD_DIM ** -0.5
    nd = len(DILS)

    w_in_b, w_out_b, w_glu_b = w_in.astype(BF16), w_out.astype(BF16), w_glu.astype(BF16)
    w_gate_b, w_up_b, w_down_b = w_gate.astype(BF16), w_up.astype(BF16), w_down.astype(BF16)
    wr_t = jnp.swapaxes(w_router, 1, 2)
    wr_hi = wr_t.astype(BF16)
    wr_lo = (wr_t - wr_hi.astype(F32)).astype(BF16)

    xz, colblk = x.reshape(b * s, d), 0
    for l in range(depth):
        hg = jnp.stack([_tile2(q_norm_a[l]) * scale, _tile2(k_norm_a[l]),
                        _tile2(q_norm_b[l]) * scale, _tile2(k_norm_b[l])], axis=0)
        outs = _in_proj(xz, colblk, b, s, attn_norm[l][None], w_in_b[l], hg, tm)
        qa, ka, va = outs[0:nd], outs[nd:2 * nd], outs[2 * nd:3 * nd]
        qb, kb, vb, u2 = outs[3 * nd:]

        oa = _dilated(qa, ka, va)
        ob = _natten(qb, kb, vb, _na_bias(rel_pos_bias[l]))

        pf = _s5_params(s5_a_re[l, 0], s5_a_im[l, 0], s5_log_dt[l, 0], s5_b_re[l, 0], s5_b_im[l, 0],
                        s5_c_re[l, 0], s5_c_im[l, 0])
        pb = _s5_params(s5_a_re[l, 1], s5_a_im[l, 1], s5_log_dt[l, 1], s5_b_re[l, 1], s5_b_im[l, 1],
                        s5_c_re[l, 1], s5_c_im[l, 1])
        avec = jnp.broadcast_to(jnp.stack([pf[0], pf[1], pb[0], pb[1]])[:, None, :],
                                (4, SUBLANES, pf[0].shape[0]))
        yf, yb = _s5(u2, pf[2], pb[2], pf[3], pb[3], avec, steps)

        z, aff = _out_proj(oa, ob, yf, yb, u2, xz, colblk,
                           out_norm_a[l][None], out_norm_b[l][None], out_norm_c[l][None], s5_d[l][None],
                           w_glu_b[l], w_out_b[l], ffn_norm[l][None], wr_hi[l], wr_lo[l], tm)

        idx, gate = _route(aff[:b], cap)
        xz = _moe(idx, gate, z, ffn_norm[l][None], w_gate_b, w_up_b, w_down_b, l, s)
        colblk = 1
    return xz[:b * s, d:].reshape(b, s, d)
```

```python
import functools

import numpy as np
import jax
import jax.numpy as jnp
from jax import lax
from jax.experimental import pallas as pl
from jax.experimental.pallas import tpu as pltpu

HEAD_DIM = 64
A_HEADS = 8
B_HEADS = 4
DILATED_CONFIGS = ((128, 1), (512, 4), (2048, 16))
GRID_W = 64
NA_ROWS = 8
NA_COLS = 16
S5_GROUP = 16
S5_STATE = 64
N_EXPERTS = 16
EC_CAPACITY = 2
EPS = 1e-6
NEG_INF = -1e30

LANES = 128
SUBLANES = 8
VMEM_LIMIT = 56 * 1024 * 1024

HALF = 64
QBLK = 2 * HALF
KBLK = 4 * HALF
DIL_CHUNK = 2048
DIL_GROUP = 4
DILS = tuple(d for _, d in DILATED_CONFIGS)
assert all(w == 2 * HALF * d for w, d in DILATED_CONFIGS) and DILS[0] == 1

F32 = jnp.float32
BF16 = jnp.bfloat16

_NT = (((1,), (1,)), ((), ()))


def _params(sem, vmem=VMEM_LIMIT, **kw):
    return pltpu.CompilerParams(dimension_semantics=sem, vmem_limit_bytes=vmem, **kw)


def _lo_lanes():
    return lax.broadcasted_iota(jnp.int32, (1, LANES), 1) < HEAD_DIM


def _stack_heads(q2, lo):
    zero = jnp.zeros_like(q2)
    return jnp.concatenate([jnp.where(lo, q2, zero), jnp.where(lo, zero, q2)], axis=0)


def _rms(t, gain):
    return (t * lax.rsqrt(jnp.mean(t * t, axis=-1, keepdims=True) + EPS)) * gain


def _in_proj_kernel(x_ref, g_ref, w_ref, hg_ref, *rest, a_pairs, b_pairs, tm, nsplit):
    nd = len(DILS)
    qa_refs, ka_refs, va_refs = rest[0:nd], rest[nd:2 * nd], rest[2 * nd:3 * nd]
    qb_ref, kb_ref, vb_ref, u_ref, scr = rest[3 * nd:]
    lo = _lo_lanes()

    def head_norm(c, gain_row):
        if gain_row is None:
            return c
        sq = c * c
        s_lo = jnp.sum(jnp.where(lo, sq, 0.0), axis=-1, keepdims=True)
        s_hi = jnp.sum(jnp.where(lo, 0.0, sq), axis=-1, keepdims=True)
        r = jnp.where(lo, lax.rsqrt(s_lo / HEAD_DIM + EPS), lax.rsqrt(s_hi / HEAD_DIM + EPS))
        return (c * r) * hg_ref[gain_row:gain_row + 1, :]

    sub = tm // nsplit
    for st in range(nsplit):
        rows = slice(st * sub, (st + 1) * sub)
        h = _rms(x_ref[rows, :], g_ref[...])
        proj = jnp.dot(h.astype(BF16), w_ref[...], preferred_element_type=F32)
        col = 0
        nscr = 0
        for refs, gain_row in ((qa_refs, 0), (ka_refs, 1), (va_refs, None)):
            for p in range(a_pairs):
                c = head_norm(proj[:, col:col + LANES], gain_row)
                refs[0][0, p, rows, :] = c.astype(BF16)
                scr[nscr, rows, :] = c
                for ref, dil in zip(refs[1:], DILS[1:]):
                    vrows = slice(st * sub // dil, (st + 1) * sub // dil)
                    for r in range(dil):
                        ref[0, p, vrows, r * LANES:(r + 1) * LANES] = (
                            scr[nscr, pl.ds(st * sub + r, sub // dil, stride=dil), :].astype(BF16))
                nscr += 1
                col += LANES
        for ref, gain_row in ((qb_ref, 2), (kb_ref, 3), (vb_ref, None)):
            for p in range(b_pairs):
                ref[0, p, rows, :] = head_norm(proj[:, col:col + LANES], gain_row).astype(BF16)
                col += LANES
        u_ref[rows, :] = proj[:, col:]


def _in_proj(xz, colblk, b, s, g, w, hg, tm, nsplit=1):
    assert tm % (nsplit * 2 * SUBLANES * DILS[-1]) == 0
    d = g.shape[1]
    nt = s // tm
    a_pairs, b_pairs = A_HEADS // 2, B_HEADS // 2
    cw = w.shape[1] - 3 * (a_pairs + b_pairs) * LANES
    shapes_a = [jax.ShapeDtypeStruct((b, a_pairs, s // dil, dil * LANES), BF16) for dil in DILS]
    specs_a = [pl.BlockSpec((1, a_pairs, tm // dil, dil * LANES), lambda bi, i: (bi, 0, i, 0)) for dil in DILS]
    shape_b = jax.ShapeDtypeStruct((b, b_pairs, s, LANES), BF16)
    spec_b = pl.BlockSpec((1, b_pairs, tm, LANES), lambda bi, i: (bi, 0, i, 0))
    return pl.pallas_call(
        functools.partial(_in_proj_kernel, a_pairs=a_pairs, b_pairs=b_pairs, tm=tm, nsplit=nsplit),
        grid=(b, nt),
        in_specs=[pl.BlockSpec((tm, d), lambda bi, i: (bi * nt + i, colblk)),
                  pl.BlockSpec((1, d), lambda bi, i: (0, 0)),
                  pl.BlockSpec(w.shape, lambda bi, i: (0, 0)),
                  pl.BlockSpec(hg.shape, lambda bi, i: (0, 0))],
        out_specs=specs_a * 3 + [spec_b] * 3 + [pl.BlockSpec((tm, cw), lambda bi, i: (i, bi))],
        out_shape=shapes_a * 3 + [shape_b] * 3 + [jax.ShapeDtypeStruct((s, b * cw), F32)],
        scratch_shapes=[pltpu.VMEM((3 * a_pairs, tm, LANES), F32)],
        compiler_params=_params(("parallel", "parallel")),
        name="in_proj",
    )(xz, g, w, hg)


def _dilated_kernel(*refs, chunk):
    nd = len(DILS)
    o_ref, acc_ref, lse_ref = refs[8 * nd:]
    c = pl.program_id(2)
    nc = pl.num_programs(2)
    lo = _lo_lanes()
    kcol = lax.broadcasted_iota(jnp.int32, (1, KBLK), 1)
    edge_first = jnp.where((kcol < HALF) & (c == 0), NEG_INF, 0.0)
    edge_last = jnp.where((kcol >= KBLK - HALF) & (c == nc - 1), NEG_INF, 0.0)
    for bi, dil in enumerate(DILS):
        q_ref, kp_ref, k_ref, kn_ref, vp_ref, v_ref, vn_ref, bias_ref = refs[8 * bi:8 * bi + 8]
        nblk = chunk // dil // QBLK
        bias = bias_ref[0]
        jobs = [(r, m) for r in range(dil) for m in range(nblk)]
        bands = {}
        for g0 in range(0, len(jobs), DIL_GROUP):
            group = jobs[g0:g0 + DIL_GROUP]
            scores, values = [], []
            for r, m in group:
                cs = slice(r * LANES, (r + 1) * LANES)
                if r not in bands:
                    bands[r] = tuple(jnp.concatenate([a[0, 0, :, cs], b_[0, 0, :, cs], c_[0, 0, :, cs]], axis=0)
                                     for a, b_, c_ in ((kp_ref, k_ref, kn_ref), (vp_ref, v_ref, vn_ref)))
                kc, vc = bands[r]
                qs = _stack_heads(q_ref[0, 0, m * QBLK:(m + 1) * QBLK, cs], lo)
                s = lax.dot_general(qs, kc[m * QBLK:m * QBLK + KBLK], _NT, preferred_element_type=F32) + bias
                if m == 0:
                    s = s + edge_first
                if m == nblk - 1:
                    s = s + edge_last
                scores.append(s)
                values.append(vc[m * QBLK:m * QBLK + KBLK])
            soft = []
            for s in scores:
                mx = jnp.max(s, axis=-1, keepdims=True)
                p = jnp.exp(s - mx)
                soft.append((p.astype(BF16), mx, jnp.sum(p, axis=-1, keepdims=True)))
            for (r, m), (p, mx, den), vv in zip(group, soft, values):
                o = jnp.dot(p, vv, preferred_element_type=F32) / den
                lse = mx + jnp.log(den)
                o2 = jnp.where(lo, o[:QBLK], o[QBLK:])
                l2 = jnp.where(lo, lse[:QBLK], lse[QBLK:])
                tok = pl.ds(m * QBLK * dil + r, QBLK, stride=dil) if dil > 1 else pl.ds(m * QBLK, QBLK)
                if bi == 0:
                    acc_ref[tok, :] = o2
                    lse_ref[tok, :] = l2
                else:
                    a_old, l_old = acc_ref[tok, :], lse_ref[tok, :]
                    m2 = jnp.maximum(l_old, l2)
                    e_old, e_new = jnp.exp(l_old - m2), jnp.exp(l2 - m2)
                    tot = e_old + e_new
                    acc_ref[tok, :] = (a_old * e_old + o2 * e_new) / tot
                    if bi < nd - 1:
                        lse_ref[tok, :] = m2 + jnp.log(tot)
    o_ref[0, 0] = acc_ref[...].astype(BF16)


def _alibi_slopes(n):
    return np.array([2.0 ** (-8.0 * (i + 1) / n) for i in range(n)], dtype=np.float32)


def _dilated_bias(dil):
    rel = (np.arange(KBLK)[None, :] - HALF) - np.arange(QBLK)[:, None]
    valid = np.abs(rel) <= HALF
    dist = (dil * np.abs(rel)).astype(np.float32)
    slopes = _alibi_slopes(A_HEADS)
    per_head = np.where(valid[None], -(slopes[:, None, None] * dist[None]), np.float32(NEG_INF)).astype(np.float32)
    return per_head.reshape(A_HEADS // 2, 2 * QBLK, KBLK)


def _dilated(q_views, k_views, v_views):
    b, hp, s, _ = q_views[0].shape
    chunk = min(DIL_CHUNK, s)
    assert s % chunk == 0 and chunk % (QBLK * DILS[-1]) == 0
    args, specs = [], []
    for q, k, v, dil in zip(q_views, k_views, v_views, DILS):
        rows, width = chunk // dil, dil * LANES
        hb = rows // HALF
        last = s // dil // HALF - 1
        main = pl.BlockSpec((1, 1, rows, width), lambda bi, h, c: (bi, h, c, 0))
        prev = pl.BlockSpec((1, 1, HALF, width), lambda bi, h, c, hb=hb: (bi, h, jnp.maximum(c * hb - 1, 0), 0))
        nxt = pl.BlockSpec((1, 1, HALF, width),
                           lambda bi, h, c, hb=hb, last=last: (bi, h, jnp.minimum((c + 1) * hb, last), 0))
        args += [q, k, k, k, v, v, v, jnp.asarray(_dilated_bias(dil))]
        specs += [main, prev, main, nxt, prev, main, nxt,
                  pl.BlockSpec((1, 2 * QBLK, KBLK), lambda bi, h, c: (h, 0, 0))]
    return pl.pallas_call(
        functools.partial(_dilated_kernel, chunk=chunk),
        grid=(b, hp, s // chunk),
        in_specs=specs,
        out_specs=pl.BlockSpec((1, 1, chunk, LANES), lambda bi, h, c: (bi, h, c, 0)),
        out_shape=jax.ShapeDtypeStruct((b, hp, s, LANES), BF16),
        scratch_shapes=[pltpu.VMEM((chunk, LANES), F32), pltpu.VMEM((chunk, LANES), F32)],
        compiler_params=_params(("parallel", "parallel", "parallel")),
        name="dilated",
    )(*args)


def _na_kernel(q_ref, k_ref, v_ref, bias_ref, o_ref, *, rb, nrows):
    i = pl.program_id(2)
    lo = _lo_lanes()
    kspan = NA_ROWS * GRID_W
    scores, values = [], []
    for j in range(rb):
        r = i * rb + j
        rs = jnp.clip(r - NA_ROWS // 2, 0, nrows - NA_ROWS)
        start = pl.multiple_of(rs * GRID_W, GRID_W)
        kk = k_ref[0, 0, pl.ds(start, kspan), :]
        values.append(v_ref[0, 0, pl.ds(start, kspan), :])
        qs = _stack_heads(q_ref[0, 0, j * GRID_W:(j + 1) * GRID_W, :], lo)
        scores.append(lax.dot_general(qs, kk, _NT, preferred_element_type=F32) + bias_ref[0, r - rs])
    probs, dens = [], []
    for s in scores:
        p = jnp.exp(s - jnp.max(s, axis=-1, keepdims=True))
        dens.append(jnp.sum(p, axis=-1, keepdims=True))
        probs.append(p.astype(BF16))
    for j in range(rb):
        o = jnp.dot(probs[j], values[j], preferred_element_type=F32) / dens[j]
        o_ref[0, 0, j * GRID_W:(j + 1) * GRID_W, :] = jnp.where(lo, o[:GRID_W], o[GRID_W:]).astype(BF16)


def _na_bias(rpb):
    t = np.arange(NA_ROWS, dtype=np.int32)[:, None, None, None]
    krow = np.arange(NA_ROWS, dtype=np.int32)[None, None, :, None]
    qc = np.arange(GRID_W, dtype=np.int32)[None, :, None, None]
    kc = np.arange(GRID_W, dtype=np.int32)[None, None, None, :]
    dr = np.broadcast_to(krow - t + NA_ROWS - 1, (NA_ROWS, GRID_W, NA_ROWS, GRID_W))
    dc = np.broadcast_to(np.clip(kc - qc + NA_COLS - 1, 0, 2 * NA_COLS - 2), dr.shape)
    cs = np.clip(qc - NA_COLS // 2, 0, GRID_W - NA_COLS)
    valid = np.broadcast_to((kc >= cs) & (kc < cs + NA_COLS), dr.shape)
    row_sel = np.equal(dr[:, 0, :, 0][..., None], np.arange(2 * NA_ROWS - 1)).astype(np.float32)
    col_sel = np.equal(dc[0, :, 0, :][..., None], np.arange(2 * NA_COLS - 1)).astype(np.float32)
    tab = jnp.einsum('hab,qcb->haqc', rpb.astype(F32), col_sel, precision=lax.Precision.HIGHEST)
    tab = jnp.einsum('haqc,tka->htqkc', tab, row_sel, precision=lax.Precision.HIGHEST)
    tab = jnp.where(valid[None], tab, NEG_INF).reshape(B_HEADS // 2, 2, NA_ROWS, GRID_W, NA_ROWS * GRID_W)
    return tab.transpose(0, 2, 1, 3, 4).reshape(B_HEADS // 2, NA_ROWS, 2 * GRID_W, NA_ROWS * GRID_W)


def _natten(q, k, v, bias, rb=16):
    b, hp, s, _ = q.shape
    nrows = s // GRID_W
    assert nrows >= NA_ROWS and nrows % rb == 0
    full = pl.BlockSpec((1, 1, s, LANES), lambda bi, h, i: (bi, h, 0, 0))
    tile = pl.BlockSpec((1, 1, rb * GRID_W, LANES), lambda bi, h, i: (bi, h, i, 0))
    return pl.pallas_call(
        functools.partial(_na_kernel, rb=rb, nrows=nrows),
        grid=(b, hp, nrows // rb),
        in_specs=[tile, full, full,
                  pl.BlockSpec((1,) + bias.shape[1:], lambda bi, h, i: (h, 0, 0, 0))],
        out_specs=tile,
        out_shape=jax.ShapeDtypeStruct(q.shape, BF16),
        compiler_params=_params(("parallel", "parallel", "arbitrary")),
        name="natten",
    )(q, k, v, bias)


def _s5_kernel(uf_ref, ub_ref, bf_ref, bb_ref, cf_ref, cb_ref, a_ref, yf_ref, yb_ref,
               xf_ref, xb_ref, io_ref, st_ref, *, steps, nstate, cw):
    @pl.when(pl.program_id(0) == 0)
    def _():
        st_ref[...] = jnp.zeros_like(st_ref)

    nlt = cw // LANES

    def time_major(u_ref):
        for bi in range(SUBLANES):
            for j in range(nlt):
                col = bi * cw + j * LANES
                io_ref[j, pl.ds(bi, steps, stride=SUBLANES), :] = u_ref[:, col:col + LANES]
        return jnp.concatenate([io_ref[j] for j in range(nlt)], axis=-1).astype(BF16)

    xf_ref[...] = jnp.dot(time_major(uf_ref), bf_ref[...], preferred_element_type=F32)
    xb_ref[...] = jnp.dot(time_major(ub_ref), bb_ref[...], preferred_element_type=F32)
    re, im = slice(0, nstate), slice(nstate, 2 * nstate)

    def step(t, carry):
        xrf, xif, xrb, xib = carry
        rf = pl.ds(pl.multiple_of(t * SUBLANES, SUBLANES), SUBLANES)
        rb = pl.ds(pl.multiple_of((steps - 1 - t) * SUBLANES, SUBLANES), SUBLANES)
        nrf = a_ref[0] * xrf - a_ref[1] * xif + xf_ref[rf, re]
        nif = a_ref[0] * xif + a_ref[1] * xrf + xf_ref[rf, im]
        nrb = a_ref[2] * xrb - a_ref[3] * xib + xb_ref[rb, re]
        nib = a_ref[2] * xib + a_ref[3] * xrb + xb_ref[rb, im]
        xf_ref[rf, re] = nrf
        xf_ref[rf, im] = nif
        xb_ref[rb, re] = nrb
        xb_ref[rb, im] = nib
        return nrf, nif, nrb, nib

    def step_pair(i, carry):
        return step(2 * i + 1, step(2 * i, carry))

    carry = lax.fori_loop(0, steps // 2, step_pair, (st_ref[0], st_ref[1], st_ref[2], st_ref[3]))
    for n, val in enumerate(carry):
        st_ref[n] = val

    def batch_major(x_ref, c_ref, y_ref):
        y = jnp.dot(x_ref[...].astype(BF16), c_ref[...], preferred_element_type=F32)
        for j in range(nlt):
            io_ref[j] = y[:, j * LANES:(j + 1) * LANES]
        for bi in range(SUBLANES):
            for j in range(nlt):
                col = bi * cw + j * LANES
                y_ref[:, col:col + LANES] = io_ref[j, pl.ds(bi, steps, stride=SUBLANES), :]

    batch_major(xf_ref, cf_ref, yf_ref)
    batch_major(xb_ref, cb_ref, yb_ref)


def _s5_params(a_re, a_im, log_dt, b_re, b_im, c_re, c_im):
    groups = a_re.shape[0]
    dt = jnp.exp(log_dt.astype(F32))[:, None]
    a = jnp.minimum(a_re.astype(F32), -1e-4)
    w = a_im.astype(F32)
    mag = jnp.exp(a * dt)
    abar_r, abar_i = mag * jnp.cos(w * dt), mag * jnp.sin(w * dt)
    den = a * a + w * w
    zr = abar_r - 1.0
    gr = (zr * a + abar_i * w) / den
    gi = (abar_i * a - zr * w) / den
    br, bi = b_re.astype(F32), b_im.astype(F32)
    bbar_r = gr[..., None] * br - gi[..., None] * bi
    bbar_i = gr[..., None] * bi + gi[..., None] * br
    eye = jnp.eye(groups, dtype=F32)
    n = groups * S5_STATE

    def in_mat(t):
        return jnp.einsum('gpc,gh->gchp', t, eye).reshape(groups * S5_GROUP, n)

    def out_mat(t):
        return jnp.einsum('gcp,gh->gphc', t, eye).reshape(n, groups * S5_GROUP)

    bmat = jnp.concatenate([in_mat(bbar_r), in_mat(bbar_i)], axis=1).astype(BF16)
    cmat = jnp.concatenate([out_mat(c_re.astype(F32)), out_mat(-c_im.astype(F32))], axis=0).astype(BF16)
    return abar_r.reshape(n), abar_i.reshape(n), bmat, cmat


def _s5(u2, bmat_f, bmat_b, cmat_f, cmat_b, avec, steps):
    s, width = u2.shape
    cw = width // SUBLANES
    nstate = avec.shape[-1]
    n = s // steps
    tr = steps * SUBLANES
    const = lambda shape: pl.BlockSpec(shape, lambda i: (0,) * len(shape))
    return pl.pallas_call(
        functools.partial(_s5_kernel, steps=steps, nstate=nstate, cw=cw),
        grid=(n,),
        in_specs=[pl.BlockSpec((steps, width), lambda i: (i, 0)),
                  pl.BlockSpec((steps, width), lambda i: (n - 1 - i, 0)),
                  const(bmat_f.shape), const(bmat_b.shape), const(cmat_f.shape), const(cmat_b.shape),
                  const(avec.shape)],
        out_specs=[pl.BlockSpec((steps, width), lambda i: (i, 0)),
                   pl.BlockSpec((steps, width), lambda i: (n - 1 - i, 0))],
        out_shape=[jax.ShapeDtypeStruct(u2.shape, F32), jax.ShapeDtypeStruct(u2.shape, F32)],
        scratch_shapes=[pltpu.VMEM((tr, 2 * nstate), F32), pltpu.VMEM((tr, 2 * nstate), F32),
                        pltpu.VMEM((cw // LANES, tr, LANES), F32), pltpu.VMEM((4, SUBLANES, nstate), F32)],
        compiler_params=_params(("arbitrary",)),
        name="s5_scan",
    )(u2, u2, bmat_f, bmat_b, cmat_f, cmat_b, avec)


def _out_proj_kernel(oa_ref, ob_ref, yf_ref, yb_ref, u_ref, x_ref,
                     ga_ref, gb_ref, gc_ref, dsk_ref, wglu_ref, wout_ref, gf_ref, wrh_ref, wrl_ref,
                     z_ref, aff_ref, *, a_pairs, b_pairs):
    oa = jnp.concatenate([oa_ref[0, p].astype(F32) for p in range(a_pairs)], axis=-1)
    ob = jnp.concatenate([ob_ref[0, p].astype(F32) for p in range(b_pairs)], axis=-1)
    y = (yf_ref[...] + yb_ref[...]) + dsk_ref[...] * u_ref[...]
    g = jax.nn.gelu(y)
    oc = g * jax.nn.sigmoid(jnp.dot(g.astype(BF16), wglu_ref[...], preferred_element_type=F32))
    mix = jnp.concatenate([_rms(oa, ga_ref[...]), _rms(ob, gb_ref[...]), _rms(oc, gc_ref[...])], axis=-1)
    x1 = x_ref[...] + jnp.dot(mix.astype(BF16), wout_ref[...], preferred_element_type=F32)
    d = x1.shape[1]
    z_ref[:, 0:d] = x1
    z_ref[:, d:2 * d] = x1
    h = _rms(x1, gf_ref[...])
    h_hi = h.astype(BF16)
    h_lo = (h - h_hi.astype(F32)).astype(BF16)
    logits = (lax.dot_general(wrh_ref[...], h_hi, _NT, preferred_element_type=F32)
              + lax.dot_general(wrh_ref[...], h_lo, _NT, preferred_element_type=F32)
              + lax.dot_general(wrl_ref[...], h_hi, _NT, preferred_element_type=F32))
    e = jnp.exp(logits - jnp.max(logits, axis=0, keepdims=True))
    aff_ref[0] = e / jnp.sum(e, axis=0, keepdims=True)


def _out_proj(oa, ob, yf, yb, u2, xz, colblk, ga, gb, gc, dsk, wglu, wout, gf, wrh, wrl, tm):
    b, a_pairs, s, _ = oa.shape
    b_pairs = ob.shape[1]
    d = wout.shape[1]
    nt = s // tm
    cw = u2.shape[1] // b
    ne = wrh.shape[0]
    src = lambda bi: jnp.minimum(bi, b - 1)
    spec_a = pl.BlockSpec((1, a_pairs, tm, LANES), lambda bi, i: (src(bi), 0, i, 0))
    spec_b = pl.BlockSpec((1, b_pairs, tm, LANES), lambda bi, i: (src(bi), 0, i, 0))
    spec_c = pl.BlockSpec((tm, cw), lambda bi, i: (i, src(bi)))
    const = lambda a: pl.BlockSpec(a.shape, lambda bi, i: (0,) * a.ndim)
    consts = (ga, gb, gc, dsk, wglu, wout, gf, wrh, wrl)
    return pl.pallas_call(
        functools.partial(_out_proj_kernel, a_pairs=a_pairs, b_pairs=b_pairs),
        grid=(b + 1, nt),
        in_specs=[spec_a, spec_b, spec_c, spec_c, spec_c,
                  pl.BlockSpec((tm, d), lambda bi, i: (src(bi) * nt + i, colblk))] + [const(a) for a in consts],
        out_specs=[pl.BlockSpec((tm, 2 * d), lambda bi, i: (bi * nt + i, 0)),
                   pl.BlockSpec((1, ne, tm), lambda bi, i: (bi, 0, i))],
        out_shape=[jax.ShapeDtypeStruct(((b + 1) * s, 2 * d), F32), jax.ShapeDtypeStruct((b + 1, ne, s), F32)],
        compiler_params=_params(("arbitrary", "arbitrary")),
        name="out_proj",
    )(oa, ob, yf, yb, u2, xz, *consts)


F32_VALUE_BITS = 31
GATE_PARTS = 3
TOK_RADIX = 64


def _select_kernel(aff_ref, posm_ref, offs_ref, *, cap, ntiles):
    a = aff_ref[0]
    ne = a.shape[0]
    bits = pltpu.bitcast(a, jnp.int32)

    def refine(i, prefix):
        cand = prefix | (jnp.int32(1) << (F32_VALUE_BITS - 1 - i))
        cnt = jnp.sum(jnp.where(bits >= cand, 1.0, 0.0), axis=1, keepdims=True)
        return jnp.where(cnt >= cap, cand, prefix)

    thr = lax.fori_loop(0, F32_VALUE_BITS, refine, jnp.zeros((ne, 1), jnp.int32))
    gt = bits > thr
    eq = bits == thr
    need = cap - jnp.sum(jnp.where(gt, 1.0, 0.0), axis=1, keepdims=True)

    row = lax.broadcasted_iota(jnp.int32, (LANES, LANES), 0)
    colm = lax.broadcasted_iota(jnp.int32, (LANES, LANES), 1)
    incl = jnp.where(row <= colm, 1.0, 0.0).astype(BF16)
    wmat = jnp.concatenate([incl, jnp.ones((LANES, LANES), BF16)], axis=1)
    lane = lax.broadcasted_iota(jnp.int32, (ne, LANES), 1)
    off_gt = jnp.zeros((ne, LANES), F32)
    off_eq = jnp.zeros((ne, LANES), F32)
    offs = jnp.zeros((ne, LANES), F32)
    for j in range(ntiles):
        sl = slice(j * LANES, (j + 1) * LANES)
        g = jnp.where(gt[:, sl], 1.0, 0.0)
        q = jnp.where(eq[:, sl], 1.0, 0.0)
        r = jnp.dot(jnp.concatenate([g, q], axis=0).astype(BF16), wmat, preferred_element_type=F32)
        before_gt = off_gt + r[:ne, :LANES] - g
        before_eq = off_eq + r[ne:, :LANES] - q
        sel = gt[:, sl] | (eq[:, sl] & (before_eq < need))
        posm_ref[0, :, sl] = jnp.where(sel, before_gt + jnp.minimum(before_eq, need), -1.0)
        offs = jnp.where(lane == j, off_gt + jnp.minimum(off_eq, need), offs)
        off_gt = off_gt + r[:ne, LANES:]
        off_eq = off_eq + r[ne:, LANES:]
    offs_ref[0] = offs.astype(jnp.int32)


def _compact_kernel(offs_ref, posm_ref, tok_ref, idx_ref, gate_ref, acc_ref, *, cap, ntiles):
    b = pl.program_id(0)
    ne = posm_ref.shape[1]
    win = 2 * LANES
    acc_ref[...] = jnp.zeros_like(acc_ref)
    srow = lax.broadcasted_iota(jnp.int32, (win, 1), 0).astype(F32)

    def tile(j, carry):
        t0 = pl.multiple_of(j * LANES, LANES)
        pos = posm_ref[0, :, pl.ds(t0, LANES)]
        bases, onehots = [], []
        for e in range(ne):
            base = pl.multiple_of((offs_ref[(b * ne + e) * LANES + j] // LANES) * LANES, LANES)
            rel = pos[e:e + 1, :] - base.astype(F32)
            onehots.append(jnp.where(rel == srow, 1.0, 0.0).astype(BF16))
            bases.append(base)
        r = jnp.dot(jnp.concatenate(onehots, axis=0), tok_ref[0, pl.ds(t0, LANES), :],
                    preferred_element_type=F32)
        for e in range(ne):
            rows = pl.ds(bases[e], win)
            acc_ref[e, rows, :] = acc_ref[e, rows, :] + r[e * win:(e + 1) * win]
        return carry

    lax.fori_loop(0, ntiles, tile, 0)
    for e in range(ne):
        a = acc_ref[e, 0:cap, :]
        idx_ref[0, e] = (a[:, 0:1] * TOK_RADIX + a[:, 1:2]).astype(jnp.int32)
        gate = a[:, ne + e:ne + e + 1]
        for part in range(1, GATE_PARTS):
            gate = gate + a[:, (1 + part) * ne + e:(1 + part) * ne + e + 1]
        gate_ref[0, e] = gate


def _token_table(aff):
    b, ne, s = aff.shape
    t = jnp.arange(s, dtype=jnp.int32)
    assert s <= TOK_RADIX * 256
    assert ne >= 2 and LANES % ne == 0 and (1 + GATE_PARTS) * ne <= LANES
    lane = jnp.arange(ne, dtype=jnp.int32)[None, :]
    ids = jnp.where(lane == 0, (t // TOK_RADIX)[:, None], jnp.where(lane == 1, (t % TOK_RADIX)[:, None], 0))
    groups = [jnp.broadcast_to(ids.astype(BF16)[None], (b, s, ne))]
    rest = jnp.swapaxes(aff, 1, 2)
    for _ in range(GATE_PARTS):
        piece = rest.astype(BF16)
        groups.append(piece)
        rest = rest - piece.astype(F32)
    groups += [jnp.zeros((b, s, ne), BF16)] * (LANES // ne - len(groups))
    return jnp.stack(groups, axis=2).reshape(b, s, LANES)


def _route(aff, cap):
    b, ne, s = aff.shape
    ntiles = s // LANES
    assert ntiles <= LANES and s % LANES == 0
    row_spec = pl.BlockSpec((1, ne, s), lambda bi: (bi, 0, 0))
    posm, offs = pl.pallas_call(
        functools.partial(_select_kernel, cap=cap, ntiles=ntiles),
        grid=(b,),
        in_specs=[row_spec],
        out_specs=[row_spec, pl.BlockSpec((1, ne, LANES), lambda bi: (bi, 0, 0))],
        out_shape=[jax.ShapeDtypeStruct((b, ne, s), F32), jax.ShapeDtypeStruct((b, ne, LANES), jnp.int32)],
        compiler_params=_params(("parallel",)),
        name="route_select",
    )(aff)
    slot_spec = pl.BlockSpec((1, ne, cap, 1), lambda bi, offs: (bi, 0, 0, 0))
    idx, gate = pl.pallas_call(
        functools.partial(_compact_kernel, cap=cap, ntiles=ntiles),
        grid_spec=pltpu.PrefetchScalarGridSpec(
            num_scalar_prefetch=1,
            grid=(b,),
            in_specs=[pl.BlockSpec((1, ne, s), lambda bi, offs: (bi, 0, 0)),
                      pl.BlockSpec((1, s, LANES), lambda bi, offs: (bi, 0, 0))],
            out_specs=[slot_spec, slot_spec],
            scratch_shapes=[pltpu.VMEM((ne, cap + 2 * LANES, LANES), F32)]),
        out_shape=[jax.ShapeDtypeStruct((b, ne, cap, 1), jnp.int32), jax.ShapeDtypeStruct((b, ne, cap, 1), F32)],
        compiler_params=_params(("parallel",)),
        name="route_compact",
    )(offs.reshape(-1), posm, _token_table(aff))
    return idx, gate


ROW_UNROLL = 8


def _moe_kernel(idx_ref, gate_ref, gf_ref, wg_ref, wu_ref, wd_ref, z_in_ref, z_ref,
                gbuf, obuf, hbuf, gsem, ssem, *, ne, nb, cap, seq, d, fchunk):
    del z_in_ref
    e, b = pl.program_id(0), pl.program_id(1)
    n = e * nb + b
    slot = n % 2
    other = 1 - slot

    def rows_of(step):
        e2, b2 = step // nb, step % nb
        return (b2 * ne + e2) * cap, b2 * seq

    def row_copies(step, fn):
        base, tok0 = rows_of(step)

        def body(i, carry):
            for k in range(ROW_UNROLL):
                r = i * ROW_UNROLL + k
                fn(r, idx_ref[base + r] + tok0)
            return carry

        lax.fori_loop(0, cap // ROW_UNROLL, body, 0)

    def gather_copy(sl, r, tok):
        return pltpu.make_async_copy(z_ref.at[pl.ds(tok, 1), :], gbuf.at[sl, pl.ds(r, 1), :], gsem.at[sl])

    def scatter_copy(sl, r, tok):
        return pltpu.make_async_copy(obuf.at[sl, pl.ds(r, 1), :], z_ref.at[pl.ds(tok, 1), pl.ds(d, d)], ssem)

    def wait_scatter(sl):
        pltpu.make_async_copy(obuf.at[sl], z_ref.at[pl.ds(0, cap), pl.ds(d, d)], ssem).wait()

    def wait_gather(sl):
        pltpu.make_async_copy(z_ref.at[pl.ds(0, cap), :], gbuf.at[sl], gsem.at[sl]).wait()

    total = ne * nb
    first, last = n == 0, n == total - 1

    @pl.when(first)
    def _():
        row_copies(0, lambda r, tok: gather_copy(0, r, tok).start())
        obuf[1] = jnp.zeros((cap, d), F32)

    wait_gather(slot)
    hbuf[...] = _rms(gbuf[slot, :, 0:d], gf_ref[...]).astype(BF16)
    obuf[slot] = jnp.zeros((cap, d), F32)
    nf = wg_ref.shape[2] // fchunk
    per = cap // nf

    base_p, tok0_p = rows_of(jnp.maximum(n - 1, 0))
    tok0_p = jnp.where(first, nb * seq, tok0_p)
    base_n, tok0_n = rows_of(jnp.minimum(n + 1, total - 1))
    for i in range(nf):
        fs = slice(i * fchunk, (i + 1) * fchunk)
        h = hbuf[...]
        g = jnp.dot(h, wg_ref[0, :, fs], preferred_element_type=F32)
        up = jnp.dot(h, wu_ref[0, :, fs], preferred_element_type=F32)
        hid = (jax.nn.silu(g) * up).astype(BF16)
        obuf[slot] = obuf[slot] + jnp.dot(hid, wd_ref[0, fs, :], preferred_element_type=F32)
        for r in range(i * per, (i + 1) * per):
            gather_copy(other, r, idx_ref[base_n + r] + tok0_n).start()
            scatter_copy(other, r, idx_ref[base_p + r] + tok0_p).start()
    wait_scatter(other)
    obuf[slot] = gbuf[slot, :, d:2 * d] + obuf[slot] * gate_ref[0, 0]

    @pl.when(last)
    def _():
        wait_gather(other)
        row_copies(n, lambda r, tok: scatter_copy(slot, r, tok).start())
        wait_scatter(slot)


def _moe(idx, gate, z, gf, wg, wu, wd, layer, seq, fchunk=512):
    b, ne, cap, _ = idx.shape
    d, ff = wg.shape[2], wg.shape[3]
    fchunk = min(fchunk, ff)
    assert cap % ROW_UNROLL == 0 and ff % fchunk == 0 and cap % (ff // fchunk) == 0
    assert b >= 3 and z.shape[0] == (b + 1) * seq, "z carries one pad sequence after the real ones"
    once = pl.Buffered(1)
    grid_spec = pltpu.PrefetchScalarGridSpec(
        num_scalar_prefetch=1,
        grid=(ne, b),
        in_specs=[pl.BlockSpec((1, 1, cap, 1), lambda e, bi, idx: (bi, e, 0, 0)),
                  pl.BlockSpec((1, d), lambda e, bi, idx: (0, 0)),
                  pl.BlockSpec((None, 1, d, ff), lambda e, bi, idx: (layer, e, 0, 0), pipeline_mode=once),
                  pl.BlockSpec((None, 1, d, ff), lambda e, bi, idx: (layer, e, 0, 0), pipeline_mode=once),
                  pl.BlockSpec((None, 1, ff, d), lambda e, bi, idx: (layer, e, 0, 0), pipeline_mode=once),
                  pl.BlockSpec(memory_space=pl.ANY)],
        out_specs=pl.BlockSpec(memory_space=pl.ANY),
        scratch_shapes=[pltpu.VMEM((2, cap, 2 * d), F32), pltpu.VMEM((2, cap, d), F32), pltpu.VMEM((cap, d), BF16),
                        pltpu.SemaphoreType.DMA((2,)), pltpu.SemaphoreType.DMA(())],
    )
    return pl.pallas_call(
        functools.partial(_moe_kernel, ne=ne, nb=b, cap=cap, seq=seq, d=d, fchunk=fchunk),
        grid_spec=grid_spec,
        out_shape=jax.ShapeDtypeStruct(z.shape, z.dtype),
        input_output_aliases={6: 0},
        compiler_params=_params(("arbitrary", "arbitrary"), disable_bounds_checks=True),
        name="moe",
    )(idx.reshape(-1), gate, gf, wg, wu, wd, z)


def _tile2(g):
    return jnp.concatenate([g, g], axis=-1)


def kernel(x, attn_norm, w_in, q_norm_a, k_norm_a, q_norm_b, k_norm_b, rel_pos_bias, s5_a_re, s5_a_im, s5_log_dt, s5_b_re, s5_b_im, s5_c_re, s5_c_im, s5_d, w_glu, out_norm_a, out_norm_b, out_norm_c, w_out, ffn_norm, w_router, w_gate, w_up, w_down):
    b, s, d = x.shape
    depth = w_in.shape[0]
    assert b == SUBLANES, "the S5 scan keeps one sequence per sublane"
    tm = min(512, s)
    steps = min(128, s)
    cap = EC_CAPACITY * s // N_EXPERTS
    scale = HEAD_DIM ** -0.5
    nd = len(DILS)

    w_in_b, w_out_b, w_glu_b = w_in.astype(BF16), w_out.astype(BF16), w_glu.astype(BF16)
    w_gate_b, w_up_b, w_down_b = w_gate.astype(BF16), w_up.astype(BF16), w_down.astype(BF16)
    wr_t = jnp.swapaxes(w_router, 1, 2)
    wr_hi = wr_t.astype(BF16)
    wr_lo = (wr_t - wr_hi.astype(F32)).astype(BF16)

    xz, colblk = x.reshape(b * s, d), 0
    for l in range(depth):
        hg = jnp.stack([_tile2(q_norm_a[l]) * scale, _tile2(k_norm_a[l]),
                        _tile2(q_norm_b[l]) * scale, _tile2(k_norm_b[l])], axis=0)
        outs = _in_proj(xz, colblk, b, s, attn_norm[l][None], w_in_b[l], hg, tm)
        qa, ka, va = outs[0:nd], outs[nd:2 * nd], outs[2 * nd:3 * nd]
        qb, kb, vb, u2 = outs[3 * nd:]

        oa = _dilated(qa, ka, va)
        ob = _natten(qb, kb, vb, _na_bias(rel_pos_bias[l]))

        pf = _s5_params(s5_a_re[l, 0], s5_a_im[l, 0], s5_log_dt[l, 0], s5_b_re[l, 0], s5_b_im[l, 0],
                        s5_c_re[l, 0], s5_c_im[l, 0])
        pb = _s5_params(s5_a_re[l, 1], s5_a_im[l, 1], s5_log_dt[l, 1], s5_b_re[l, 1], s5_b_im[l, 1],
                        s5_c_re[l, 1], s5_c_im[l, 1])
        avec = jnp.broadcast_to(jnp.stack([pf[0], pf[1], pb[0], pb[1]])[:, None, :],
                                (4, SUBLANES, pf[0].shape[0]))
        yf, yb = _s5(u2, pf[2], pb[2], pf[3], pb[3], avec, steps)

        z, aff = _out_proj(oa, ob, yf, yb, u2, xz, colblk,
                           out_norm_a[l][None], out_norm_b[l][None], out_norm_c[l][None], s5_d[l][None],
                           w_glu_b[l], w_out_b[l], ffn_norm[l][None], wr_hi[l], wr_lo[l], tm)

        idx, gate = _route(aff[:b], cap)
        xz = _moe(idx, gate, z, ffn_norm[l][None], w_gate_b, w_up_b, w_down_b, l, s)
        colblk = 1
    return xz[:b * s, d:].reshape(b, s, d)
```

```python
import functools

import numpy as np
import jax
import jax.numpy as jnp
from jax import lax
from jax.experimental import pallas as pl
from jax.experimental.pallas import tpu as pltpu

HEAD_DIM = 64
A_HEADS = 8
B_HEADS = 4
DILATED_CONFIGS = ((128, 1), (512, 4), (2048, 16))
GRID_W = 64
NA_ROWS = 8
NA_COLS = 16
S5_GROUP = 16
S5_STATE = 64
N_EXPERTS = 16
EC_CAPACITY = 2
EPS = 1e-6
NEG_INF = -1e30

LANES = 128
SUBLANES = 8
VMEM_LIMIT = 56 * 1024 * 1024

HALF = 64
QBLK = 2 * HALF
KBLK = 4 * HALF
DIL_CHUNK = 2048
DIL_GROUP = 4
DILS = tuple(d for _, d in DILATED_CONFIGS)
assert all(w == 2 * HALF * d for w, d in DILATED_CONFIGS) and DILS[0] == 1

F32 = jnp.float32
BF16 = jnp.bfloat16

_NT = (((1,), (1,)), ((), ()))


def _params(sem, vmem=VMEM_LIMIT, **kw):
    return pltpu.CompilerParams(dimension_semantics=sem, vmem_limit_bytes=vmem, **kw)


def _lo_lanes():
    return lax.broadcasted_iota(jnp.int32, (1, LANES), 1) < HEAD_DIM


def _stack_heads(q2, lo):
    zero = jnp.zeros_like(q2)
    return jnp.concatenate([jnp.where(lo, q2, zero), jnp.where(lo, zero, q2)], axis=0)


def _rms(t, gain):
    return (t * lax.rsqrt(jnp.mean(t * t, axis=-1, keepdims=True) + EPS)) * gain


def _in_proj_kernel(x_ref, g_ref, w_ref, hg_ref, *rest, a_pairs, b_pairs, tm, nsplit):
    nd = len(DILS)
    qa_refs, ka_refs, va_refs = rest[0:nd], rest[nd:2 * nd], rest[2 * nd:3 * nd]
    qb_ref, kb_ref, vb_ref, u_ref, scr = rest[3 * nd:]
    lo = _lo_lanes()

    def head_norm(c, gain_row):
        if gain_row is None:
            return c
        sq = c * c
        s_lo = jnp.sum(jnp.where(lo, sq, 0.0), axis=-1, keepdims=True)
        s_hi = jnp.sum(jnp.where(lo, 0.0, sq), axis=-1, keepdims=True)
        r = jnp.where(lo, lax.rsqrt(s_lo / HEAD_DIM + EPS), lax.rsqrt(s_hi / HEAD_DIM + EPS))
        return (c * r) * hg_ref[gain_row:gain_row + 1, :]

    sub = tm // nsplit
    for st in range(nsplit):
        rows = slice(st * sub, (st + 1) * sub)
        h = _rms(x_ref[rows, :], g_ref[...])
        proj = jnp.dot(h.astype(BF16), w_ref[...], preferred_element_type=F32)
        col = 0
        nscr = 0
        for refs, gain_row in ((qa_refs, 0), (ka_refs, 1), (va_refs, None)):
            for p in range(a_pairs):
                c = head_norm(proj[:, col:col + LANES], gain_row)
                refs[0][0, p, rows, :] = c.astype(BF16)
                scr[nscr, rows, :] = c
                for ref, dil in zip(refs[1:], DILS[1:]):
                    vrows = slice(st * sub // dil, (st + 1) * sub // dil)
                    for r in range(dil):
                        ref[0, p, vrows, r * LANES:(r + 1) * LANES] = (
                            scr[nscr, pl.ds(st * sub + r, sub // dil, stride=dil), :].astype(BF16))
                nscr += 1
                col += LANES
        for ref, gain_row in ((qb_ref, 2), (kb_ref, 3), (vb_ref, None)):
            for p in range(b_pairs):
                ref[0, p, rows, :] = head_norm(proj[:, col:col + LANES], gain_row).astype(BF16)
                col += LANES
        u_ref[rows, :] = proj[:, col:]


def _in_proj(xz, colblk, b, s, g, w, hg, tm, nsplit=1):
    assert tm % (nsplit * 2 * SUBLANES * DILS[-1]) == 0
    d = g.shape[1]
    nt = s // tm
    a_pairs, b_pairs = A_HEADS // 2, B_HEADS // 2
    cw = w.shape[1] - 3 * (a_pairs + b_pairs) * LANES
    shapes_a = [jax.ShapeDtypeStruct((b, a_pairs, s // dil, dil * LANES), BF16) for dil in DILS]
    specs_a = [pl.BlockSpec((1, a_pairs, tm // dil, dil * LANES), lambda bi, i: (bi, 0, i, 0)) for dil in DILS]
    shape_b = jax.ShapeDtypeStruct((b, b_pairs, s, LANES), BF16)
    spec_b = pl.BlockSpec((1, b_pairs, tm, LANES), lambda bi, i: (bi, 0, i, 0))
    return pl.pallas_call(
        functools.partial(_in_proj_kernel, a_pairs=a_pairs, b_pairs=b_pairs, tm=tm, nsplit=nsplit),
        grid=(b, nt),
        in_specs=[pl.BlockSpec((tm, d), lambda bi, i: (bi * nt + i, colblk)),
                  pl.BlockSpec((1, d), lambda bi, i: (0, 0)),
                  pl.BlockSpec(w.shape, lambda bi, i: (0, 0)),
                  pl.BlockSpec(hg.shape, lambda bi, i: (0, 0))],
        out_specs=specs_a * 3 + [spec_b] * 3 + [pl.BlockSpec((tm, cw), lambda bi, i: (i, bi))],
        out_shape=shapes_a * 3 + [shape_b] * 3 + [jax.ShapeDtypeStruct((s, b * cw), F32)],
        scratch_shapes=[pltpu.VMEM((3 * a_pairs, tm, LANES), F32)],
        compiler_params=_params(("parallel", "parallel")),
        name="in_proj",
    )(xz, g, w, hg)


def _dilated_kernel(*refs, chunk):
    nd = len(DILS)
    o_ref, acc_ref, lse_ref = refs[8 * nd:]
    c = pl.program_id(2)
    nc = pl.num_programs(2)
    lo = _lo_lanes()
    kcol = lax.broadcasted_iota(jnp.int32, (1, KBLK), 1)
    edge_first = jnp.where((kcol < HALF) & (c == 0), NEG_INF, 0.0)
    edge_last = jnp.where((kcol >= KBLK - HALF) & (c == nc - 1), NEG_INF, 0.0)
    for bi, dil in enumerate(DILS):
        q_ref, kp_ref, k_ref, kn_ref, vp_ref, v_ref, vn_ref, bias_ref = refs[8 * bi:8 * bi + 8]
        nblk = chunk // dil // QBLK
        bias = bias_ref[0]
        jobs = [(r, m) for r in range(dil) for m in range(nblk)]
        bands = {}
        for g0 in range(0, len(jobs), DIL_GROUP):
            group = jobs[g0:g0 + DIL_GROUP]
            scores, values = [], []
            for r, m in group:
                cs = slice(r * LANES, (r + 1) * LANES)
                if r not in bands:
                    bands[r] = tuple(jnp.concatenate([a[0, 0, :, cs], b_[0, 0, :, cs], c_[0, 0, :, cs]], axis=0)
                                     for a, b_, c_ in ((kp_ref, k_ref, kn_ref), (vp_ref, v_ref, vn_ref)))
                kc, vc = bands[r]
                qs = _stack_heads(q_ref[0, 0, m * QBLK:(m + 1) * QBLK, cs], lo)
                s = lax.dot_general(qs, kc[m * QBLK:m * QBLK + KBLK], _NT, preferred_element_type=F32) + bias
                if m == 0:
                    s = s + edge_first
                if m == nblk - 1:
                    s = s + edge_last
                scores.append(s)
                values.append(vc[m * QBLK:m * QBLK + KBLK])
            soft = []
            for s in scores:
                mx = jnp.max(s, axis=-1, keepdims=True)
                p = jnp.exp(s - mx)
                soft.append((p.astype(BF16), mx, jnp.sum(p, axis=-1, keepdims=True)))
            for (r, m), (p, mx, den), vv in zip(group, soft, values):
                o = jnp.dot(p, vv, preferred_element_type=F32) / den
                lse = mx + jnp.log(den)
                o2 = jnp.where(lo, o[:QBLK], o[QBLK:])
                l2 = jnp.where(lo, lse[:QBLK], lse[QBLK:])
                tok = pl.ds(m * QBLK * dil + r, QBLK, stride=dil) if dil > 1 else pl.ds(m * QBLK, QBLK)
                if bi == 0:
                    acc_ref[tok, :] = o2
                    lse_ref[tok, :] = l2
                else:
                    a_old, l_old = acc_ref[tok, :], lse_ref[tok, :]
                    m2 = jnp.maximum(l_old, l2)
                    e_old, e_new = jnp.exp(l_old - m2), jnp.exp(l2 - m2)
                    tot = e_old + e_new
                    acc_ref[tok, :] = (a_old * e_old + o2 * e_new) / tot
                    if bi < nd - 1:
                        lse_ref[tok, :] = m2 + jnp.log(tot)
    o_ref[0, 0] = acc_ref[...].astype(BF16)


def _alibi_slopes(n):
    return np.array([2.0 ** (-8.0 * (i + 1) / n) for i in range(n)], dtype=np.float32)


def _dilated_bias(dil):
    rel = (np.arange(KBLK)[None, :] - HALF) - np.arange(QBLK)[:, None]
    valid = np.abs(rel) <= HALF
    dist = (dil * np.abs(rel)).astype(np.float32)
    slopes = _alibi_slopes(A_HEADS)
    per_head = np.where(valid[None], -(slopes[:, None, None] * dist[None]), np.float32(NEG_INF)).astype(np.float32)
    return per_head.reshape(A_HEADS // 2, 2 * QBLK, KBLK)


def _dilated(q_views, k_views, v_views):
    b, hp, s, _ = q_views[0].shape
    chunk = min(DIL_CHUNK, s)
    assert s % chunk == 0 and chunk % (QBLK * DILS[-1]) == 0
    args, specs = [], []
    for q, k, v, dil in zip(q_views, k_views, v_views, DILS):
        rows, width = chunk // dil, dil * LANES
        hb = rows // HALF
        last = s // dil // HALF - 1
        main = pl.BlockSpec((1, 1, rows, width), lambda bi, h, c: (bi, h, c, 0))
        prev = pl.BlockSpec((1, 1, HALF, width), lambda bi, h, c, hb=hb: (bi, h, jnp.maximum(c * hb - 1, 0), 0))
        nxt = pl.BlockSpec((1, 1, HALF, width),
                           lambda bi, h, c, hb=hb, last=last: (bi, h, jnp.minimum((c + 1) * hb, last), 0))
        args += [q, k, k, k, v, v, v, jnp.asarray(_dilated_bias(dil))]
        specs += [main, prev, main, nxt, prev, main, nxt,
                  pl.BlockSpec((1, 2 * QBLK, KBLK), lambda bi, h, c: (h, 0, 0))]
    return pl.pallas_call(
        functools.partial(_dilated_kernel, chunk=chunk),
        grid=(b, hp, s // chunk),
        in_specs=specs,
        out_specs=pl.BlockSpec((1, 1, chunk, LANES), lambda bi, h, c: (bi, h, c, 0)),
        out_shape=jax.ShapeDtypeStruct((b, hp, s, LANES), BF16),
        scratch_shapes=[pltpu.VMEM((chunk, LANES), F32), pltpu.VMEM((chunk, LANES), F32)],
        compiler_params=_params(("parallel", "parallel", "parallel")),
        name="dilated",
    )(*args)


def _na_kernel(q_ref, k_ref, v_ref, bias_ref, o_ref, *, rb, nrows):
    i = pl.program_id(2)
    lo = _lo_lanes()
    kspan = NA_ROWS * GRID_W
    scores, values = [], []
    for j in range(rb):
        r = i * rb + j
        rs = jnp.clip(r - NA_ROWS // 2, 0, nrows - NA_ROWS)
        start = pl.multiple_of(rs * GRID_W, GRID_W)
        kk = k_ref[0, 0, pl.ds(start, kspan), :]
        values.append(v_ref[0, 0, pl.ds(start, kspan), :])
        qs = _stack_heads(q_ref[0, 0, j * GRID_W:(j + 1) * GRID_W, :], lo)
        scores.append(lax.dot_general(qs, kk, _NT, preferred_element_type=F32) + bias_ref[0, r - rs])
    probs, dens = [], []
    for s in scores:
        p = jnp.exp(s - jnp.max(s, axis=-1, keepdims=True))
        dens.append(jnp.sum(p, axis=-1, keepdims=True))
        probs.append(p.astype(BF16))
    for j in range(rb):
        o = jnp.dot(probs[j], values[j], preferred_element_type=F32) / dens[j]
        o_ref[0, 0, j * GRID_W:(j + 1) * GRID_W, :] = jnp.where(lo, o[:GRID_W], o[GRID_W:]).astype(BF16)


def _na_bias(rpb):
    t = np.arange(NA_ROWS, dtype=np.int32)[:, None, None, None]
    krow = np.arange(NA_ROWS, dtype=np.int32)[None, None, :, None]
    qc = np.arange(GRID_W, dtype=np.int32)[None, :, None, None]
    kc = np.arange(GRID_W, dtype=np.int32)[None, None, None, :]
    dr = np.broadcast_to(krow - t + NA_ROWS - 1, (NA_ROWS, GRID_W, NA_ROWS, GRID_W))
    dc = np.broadcast_to(np.clip(kc - qc + NA_COLS - 1, 0, 2 * NA_COLS - 2), dr.shape)
    cs = np.clip(qc - NA_COLS // 2, 0, GRID_W - NA_COLS)
    valid = np.broadcast_to((kc >= cs) & (kc < cs + NA_COLS), dr.shape)
    row_sel = np.equal(dr[:, 0, :, 0][..., None], np.arange(2 * NA_ROWS - 1)).astype(np.float32)
    col_sel = np.equal(dc[0, :, 0, :][..., None], np.arange(2 * NA_COLS - 1)).astype(np.float32)
    tab = jnp.einsum('hab,qcb->haqc', rpb.astype(F32), col_sel, precision=lax.Precision.HIGHEST)
    tab = jnp.einsum('haqc,tka->htqkc', tab, row_sel, precision=lax.Precision.HIGHEST)
    tab = jnp.where(valid[None], tab, NEG_INF).reshape(B_HEADS // 2, 2, NA_ROWS, GRID_W, NA_ROWS * GRID_W)
    return tab.transpose(0, 2, 1, 3, 4).reshape(B_HEADS // 2, NA_ROWS, 2 * GRID_W, NA_ROWS * GRID_W)


def _natten(q, k, v, bias, rb=16):
    b, hp, s, _ = q.shape
    nrows = s // GRID_W
    assert nrows >= NA_ROWS and nrows % rb == 0
    full = pl.BlockSpec((1, 1, s, LANES), lambda bi, h, i: (bi, h, 0, 0))
    tile = pl.BlockSpec((1, 1, rb * GRID_W, LANES), lambda bi, h, i: (bi, h, i, 0))
    return pl.pallas_call(
        functools.partial(_na_kernel, rb=rb, nrows=nrows),
        grid=(b, hp, nrows // rb),
        in_specs=[tile, full, full,
                  pl.BlockSpec((1,) + bias.shape[1:], lambda bi, h, i: (h, 0, 0, 0))],
        out_specs=tile,
        out_shape=jax.ShapeDtypeStruct(q.shape, BF16),
        compiler_params=_params(("parallel", "parallel", "arbitrary")),
        name="natten",
    )(q, k, v, bias)


def _s5_kernel(uf_ref, ub_ref, bf_ref, bb_ref, cf_ref, cb_ref, a_ref, yf_ref, yb_ref,
               xf_ref, xb_ref, io_ref, st_ref, *, steps, nstate, cw):
    @pl.when(pl.program_id(0) == 0)
    def _():
        st_ref[...] = jnp.zeros_like(st_ref)

    nlt = cw // LANES

    def time_major(u_ref):
        for bi in range(SUBLANES):
            for j in range(nlt):
                col = bi * cw + j * LANES
                io_ref[j, pl.ds(bi, steps, stride=SUBLANES), :] = u_ref[:, col:col + LANES]
        return jnp.concatenate([io_ref[j] for j in range(nlt)], axis=-1).astype(BF16)

    xf_ref[...] = jnp.dot(time_major(uf_ref), bf_ref[...], preferred_element_type=F32)
    xb_ref[...] = jnp.dot(time_major(ub_ref), bb_ref[...], preferred_element_type=F32)
    re, im = slice(0, nstate), slice(nstate, 2 * nstate)

    def step(t, carry):
        xrf, xif, xrb, xib = carry
        rf = pl.ds(pl.multiple_of(t * SUBLANES, SUBLANES), SUBLANES)
        rb = pl.ds(pl.multiple_of((steps - 1 - t) * SUBLANES, SUBLANES), SUBLANES)
        nrf = a_ref[0] * xrf - a_ref[1] * xif + xf_ref[rf, re]
        nif = a_ref[0] * xif + a_ref[1] * xrf + xf_ref[rf, im]
        nrb = a_ref[2] * xrb - a_ref[3] * xib + xb_ref[rb, re]
        nib = a_ref[2] * xib + a_ref[3] * xrb + xb_ref[rb, im]
        xf_ref[rf, re] = nrf
        xf_ref[rf, im] = nif
        xb_ref[rb, re] = nrb
        xb_ref[rb, im] = nib
        return nrf, nif, nrb, nib

    def step_pair(i, carry):
        return step(2 * i + 1, step(2 * i, carry))

    carry = lax.fori_loop(0, steps // 2, step_pair, (st_ref[0], st_ref[1], st_ref[2], st_ref[3]))
    for n, val in enumerate(carry):
        st_ref[n] = val

    def batch_major(x_ref, c_ref, y_ref):
        y = jnp.dot(x_ref[...].astype(BF16), c_ref[...], preferred_element_type=F32)
        for j in range(nlt):
            io_ref[j] = y[:, j * LANES:(j + 1) * LANES]
        for bi in range(SUBLANES):
            for j in range(nlt):
                col = bi * cw + j * LANES
                y_ref[:, col:col + LANES] = io_ref[j, pl.ds(bi, steps, stride=SUBLANES), :]

    batch_major(xf_ref, cf_ref, yf_ref)
    batch_major(xb_ref, cb_ref, yb_ref)


def _s5_params(a_re, a_im, log_dt, b_re, b_im, c_re, c_im):
    groups = a_re.shape[0]
    dt = jnp.exp(log_dt.astype(F32))[:, None]
    a = jnp.minimum(a_re.astype(F32), -1e-4)
    w = a_im.astype(F32)
    mag = jnp.exp(a * dt)
    abar_r, abar_i = mag * jnp.cos(w * dt), mag * jnp.sin(w * dt)
    den = a * a + w * w
    zr = abar_r - 1.0
    gr = (zr * a + abar_i * w) / den
    gi = (abar_i * a - zr * w) / den
    br, bi = b_re.astype(F32), b_im.astype(F32)
    bbar_r = gr[..., None] * br - gi[..., None] * bi
    bbar_i = gr[..., None] * bi + gi[..., None] * br
    eye = jnp.eye(groups, dtype=F32)
    n = groups * S5_STATE

    def in_mat(t):
        return jnp.einsum('gpc,gh->gchp', t, eye).reshape(groups * S5_GROUP, n)

    def out_mat(t):
        return jnp.einsum('gcp,gh->gphc', t, eye).reshape(n, groups * S5_GROUP)

    bmat = jnp.concatenate([in_mat(bbar_r), in_mat(bbar_i)], axis=1).astype(BF16)
    cmat = jnp.concatenate([out_mat(c_re.astype(F32)), out_mat(-c_im.astype(F32))], axis=0).astype(BF16)
    return abar_r.reshape(n), abar_i.reshape(n), bmat, cmat


def _s5(u2, bmat_f, bmat_b, cmat_f, cmat_b, avec, steps):
    s, width = u2.shape
    cw = width // SUBLANES
    nstate = avec.shape[-1]
    n = s // steps
    tr = steps * SUBLANES
    const = lambda shape: pl.BlockSpec(shape, lambda i: (0,) * len(shape))
    return pl.pallas_call(
        functools.partial(_s5_kernel, steps=steps, nstate=nstate, cw=cw),
        grid=(n,),
        in_specs=[pl.BlockSpec((steps, width), lambda i: (i, 0)),
                  pl.BlockSpec((steps, width), lambda i: (n - 1 - i, 0)),
                  const(bmat_f.shape), const(bmat_b.shape), const(cmat_f.shape), const(cmat_b.shape),
                  const(avec.shape)],
        out_specs=[pl.BlockSpec((steps, width), lambda i: (i, 0)),
                   pl.BlockSpec((steps, width), lambda i: (n - 1 - i, 0))],
        out_shape=[jax.ShapeDtypeStruct(u2.shape, F32), jax.ShapeDtypeStruct(u2.shape, F32)],
        scratch_shapes=[pltpu.VMEM((tr, 2 * nstate), F32), pltpu.VMEM((tr, 2 * nstate), F32),
                        pltpu.VMEM((cw // LANES, tr, LANES), F32), pltpu.VMEM((4, SUBLANES, nstate), F32)],
        compiler_params=_params(("arbitrary",)),
        name="s5_scan",
    )(u2, u2, bmat_f, bmat_b, cmat_f, cmat_b, avec)


def _out_proj_kernel(oa_ref, ob_ref, yf_ref, yb_ref, u_ref, x_ref,
                     ga_ref, gb_ref, gc_ref, dsk_ref, wglu_ref, wout_ref, gf_ref, wrh_ref, wrl_ref,
                     z_ref, aff_ref, *, a_pairs, b_pairs):
    oa = jnp.concatenate([oa_ref[0, p].astype(F32) for p in range(a_pairs)], axis=-1)
    ob = jnp.concatenate([ob_ref[0, p].astype(F32) for p in range(b_pairs)], axis=-1)
    y = (yf_ref[...] + yb_ref[...]) + dsk_ref[...] * u_ref[...]
    g = jax.nn.gelu(y)
    oc = g * jax.nn.sigmoid(jnp.dot(g.astype(BF16), wglu_ref[...], preferred_element_type=F32))
    mix = jnp.concatenate([_rms(oa, ga_ref[...]), _rms(ob, gb_ref[...]), _rms(oc, gc_ref[...])], axis=-1)
    x1 = x_ref[...] + jnp.dot(mix.astype(BF16), wout_ref[...], preferred_element_type=F32)
    d = x1.shape[1]
    z_ref[:, 0:d] = x1
    z_ref[:, d:2 * d] = x1
    h = _rms(x1, gf_ref[...])
    h_hi = h.astype(BF16)
    h_lo = (h - h_hi.astype(F32)).astype(BF16)
    logits = (lax.dot_general(wrh_ref[...], h_hi, _NT, preferred_element_type=F32)
              + lax.dot_general(wrh_ref[...], h_lo, _NT, preferred_element_type=F32)
              + lax.dot_general(wrl_ref[...], h_hi, _NT, preferred_element_type=F32))
    e = jnp.exp(logits - jnp.max(logits, axis=0, keepdims=True))
    aff_ref[0] = e / jnp.sum(e, axis=0, keepdims=True)


def _out_proj(oa, ob, yf, yb, u2, xz, colblk, ga, gb, gc, dsk, wglu, wout, gf, wrh, wrl, tm):
    b, a_pairs, s, _ = oa.shape
    b_pairs = ob.shape[1]
    d = wout.shape[1]
    nt = s // tm
    cw = u2.shape[1] // b
    ne = wrh.shape[0]
    src = lambda bi: jnp.minimum(bi, b - 1)
    spec_a = pl.BlockSpec((1, a_pairs, tm, LANES), lambda bi, i: (src(bi), 0, i, 0))
    spec_b = pl.BlockSpec((1, b_pairs, tm, LANES), lambda bi, i: (src(bi), 0, i, 0))
    spec_c = pl.BlockSpec((tm, cw), lambda bi, i: (i, src(bi)))
    const = lambda a: pl.BlockSpec(a.shape, lambda bi, i: (0,) * a.ndim)
    consts = (ga, gb, gc, dsk, wglu, wout, gf, wrh, wrl)
    return pl.pallas_call(
        functools.partial(_out_proj_kernel, a_pairs=a_pairs, b_pairs=b_pairs),
        grid=(b + 1, nt),
        in_specs=[spec_a, spec_b, spec_c, spec_c, spec_c,
                  pl.BlockSpec((tm, d), lambda bi, i: (src(bi) * nt + i, colblk))] + [const(a) for a in consts],
        out_specs=[pl.BlockSpec((tm, 2 * d), lambda bi, i: (bi * nt + i, 0)),
                   pl.BlockSpec((1, ne, tm), lambda bi, i: (bi, 0, i))],
        out_shape=[jax.ShapeDtypeStruct(((b + 1) * s, 2 * d), F32), jax.ShapeDtypeStruct((b + 1, ne, s), F32)],
        compiler_params=_params(("arbitrary", "arbitrary")),
        name="out_proj",
    )(oa, ob, yf, yb, u2, xz, *consts)


F32_VALUE_BITS = 31
GATE_PARTS = 3
TOK_RADIX = 64


def _select_kernel(aff_ref, posm_ref, offs_ref, *, cap, ntiles):
    a = aff_ref[0]
    ne = a.shape[0]
    bits = pltpu.bitcast(a, jnp.int32)

    def refine(i, prefix):
        cand = prefix | (jnp.int32(1) << (F32_VALUE_BITS - 1 - i))
        cnt = jnp.sum(jnp.where(bits >= cand, 1.0, 0.0), axis=1, keepdims=True)
        return jnp.where(cnt >= cap, cand, prefix)

    thr = lax.fori_loop(0, F32_VALUE_BITS, refine, jnp.zeros((ne, 1), jnp.int32))
    gt = bits > thr
    eq = bits == thr
    need = cap - jnp.sum(jnp.where(gt, 1.0, 0.0), axis=1, keepdims=True)

    row = lax.broadcasted_iota(jnp.int32, (LANES, LANES), 0)
    colm = lax.broadcasted_iota(jnp.int32, (LANES, LANES), 1)
    incl = jnp.where(row <= colm, 1.0, 0.0).astype(BF16)
    wmat = jnp.concatenate([incl, jnp.ones((LANES, LANES), BF16)], axis=1)
    lane = lax.broadcasted_iota(jnp.int32, (ne, LANES), 1)
    off_gt = jnp.zeros((ne, LANES), F32)
    off_eq = jnp.zeros((ne, LANES), F32)
    offs = jnp.zeros((ne, LANES), F32)
    for j in range(ntiles):
        sl = slice(j * LANES, (j + 1) * LANES)
        g = jnp.where(gt[:, sl], 1.0, 0.0)
        q = jnp.where(eq[:, sl], 1.0, 0.0)
        r = jnp.dot(jnp.concatenate([g, q], axis=0).astype(BF16), wmat, preferred_element_type=F32)
        before_gt = off_gt + r[:ne, :LANES] - g
        before_eq = off_eq + r[ne:, :LANES] - q
        sel = gt[:, sl] | (eq[:, sl] & (before_eq < need))
        posm_ref[0, :, sl] = jnp.where(sel, before_gt + jnp.minimum(before_eq, need), -1.0)
        offs = jnp.where(lane == j, off_gt + jnp.minimum(off_eq, need), offs)
        off_gt = off_gt + r[:ne, LANES:]
        off_eq = off_eq + r[ne:, LANES:]
    offs_ref[0] = offs.astype(jnp.int32)


def _compact_kernel(offs_ref, posm_ref, tok_ref, idx_ref, gate_ref, acc_ref, *, cap, ntiles):
    b = pl.program_id(0)
    ne = posm_ref.shape[1]
    win = 2 * LANES
    acc_ref[...] = jnp.zeros_like(acc_ref)
    srow = lax.broadcasted_iota(jnp.int32, (win, 1), 0).astype(F32)

    def tile(j, carry):
        t0 = pl.multiple_of(j * LANES, LANES)
        pos = posm_ref[0, :, pl.ds(t0, LANES)]
        bases, onehots = [], []
        for e in range(ne):
            base = pl.multiple_of((offs_ref[(b * ne + e) * LANES + j] // LANES) * LANES, LANES)
            rel = pos[e:e + 1, :] - base.astype(F32)
            onehots.append(jnp.where(rel == srow, 1.0, 0.0).astype(BF16))
            bases.append(base)
        r = jnp.dot(jnp.concatenate(onehots, axis=0), tok_ref[0, pl.ds(t0, LANES), :],
                    preferred_element_type=F32)
        for e in range(ne):
            rows = pl.ds(bases[e], win)
            acc_ref[e, rows, :] = acc_ref[e, rows, :] + r[e * win:(e + 1) * win]
        return carry

    lax.fori_loop(0, ntiles, tile, 0)
    rowi = lax.broadcasted_iota(jnp.int32, (LANES, LANES), 0)
    coli = lax.broadcasted_iota(jnp.int32, (LANES, LANES), 1)
    for e in range(ne):
        a = acc_ref[e, 0:cap, :].astype(BF16)
        w = jnp.where((coli == 0) & (rowi == 0), float(TOK_RADIX), 0.0) + jnp.where((coli == 0) & (rowi == 1), 1.0, 0.0)
        for part in range(GATE_PARTS):
            w = w + jnp.where((coli == 1) & (rowi == (1 + part) * ne + e), 1.0, 0.0)
        r = jnp.dot(a, w.astype(BF16), preferred_element_type=F32)
        idx_ref[0, e] = r[:, 0:1].astype(jnp.int32)
        gate_ref[0, e] = r[:, 1:2]


def _token_table(aff):
    b, ne, s = aff.shape
    t = jnp.arange(s, dtype=jnp.int32)
    assert s <= TOK_RADIX * 256
    assert ne >= 2 and LANES % ne == 0 and (1 + GATE_PARTS) * ne <= LANES
    lane = jnp.arange(ne, dtype=jnp.int32)[None, :]
    ids = jnp.where(lane == 0, (t // TOK_RADIX)[:, None], jnp.where(lane == 1, (t % TOK_RADIX)[:, None], 0))
    groups = [jnp.broadcast_to(ids.astype(BF16)[None], (b, s, ne))]
    rest = jnp.swapaxes(aff, 1, 2)
    for _ in range(GATE_PARTS):
        piece = rest.astype(BF16)
        groups.append(piece)
        rest = rest - piece.astype(F32)
    groups += [jnp.zeros((b, s, ne), BF16)] * (LANES // ne - len(groups))
    return jnp.stack(groups, axis=2).reshape(b, s, LANES)


def _route(aff, cap):
    b, ne, s = aff.shape
    ntiles = s // LANES
    assert ntiles <= LANES and s % LANES == 0
    row_spec = pl.BlockSpec((1, ne, s), lambda bi: (bi, 0, 0))
    posm, offs = pl.pallas_call(
        functools.partial(_select_kernel, cap=cap, ntiles=ntiles),
        grid=(b,),
        in_specs=[row_spec],
        out_specs=[row_spec, pl.BlockSpec((1, ne, LANES), lambda bi: (bi, 0, 0))],
        out_shape=[jax.ShapeDtypeStruct((b, ne, s), F32), jax.ShapeDtypeStruct((b, ne, LANES), jnp.int32)],
        compiler_params=_params(("parallel",)),
        name="route_select",
    )(aff)
    slot_spec = pl.BlockSpec((1, ne, cap, 1), lambda bi, offs: (bi, 0, 0, 0))
    idx, gate = pl.pallas_call(
        functools.partial(_compact_kernel, cap=cap, ntiles=ntiles),
        grid_spec=pltpu.PrefetchScalarGridSpec(
            num_scalar_prefetch=1,
            grid=(b,),
            in_specs=[pl.BlockSpec((1, ne, s), lambda bi, offs: (bi, 0, 0)),
                      pl.BlockSpec((1, s, LANES), lambda bi, offs: (bi, 0, 0))],
            out_specs=[slot_spec, slot_spec],
            scratch_shapes=[pltpu.VMEM((ne, cap + 2 * LANES, LANES), F32)]),
        out_shape=[jax.ShapeDtypeStruct((b, ne, cap, 1), jnp.int32), jax.ShapeDtypeStruct((b, ne, cap, 1), F32)],
        compiler_params=_params(("parallel",)),
        name="route_compact",
    )(offs.reshape(-1), posm, _token_table(aff))
    return idx, gate


ROW_UNROLL = 8


def _moe_kernel(idx_ref, gate_ref, gf_ref, wg_ref, wu_ref, wd_ref, z_in_ref, z_ref,
                gbuf, obuf, hbuf, gsem, ssem, *, ne, nb, cap, seq, d, fchunk):
    del z_in_ref
    e, b = pl.program_id(0), pl.program_id(1)
    n = e * nb + b
    slot = n % 2
    other = 1 - slot

    def rows_of(step):
        e2, b2 = step // nb, step % nb
        return (b2 * ne + e2) * cap, b2 * seq

    def row_copies(step, fn):
        base, tok0 = rows_of(step)

        def body(i, carry):
            for k in range(ROW_UNROLL):
                r = i * ROW_UNROLL + k
                fn(r, idx_ref[base + r] + tok0)
            return carry

        lax.fori_loop(0, cap // ROW_UNROLL, body, 0)

    def gather_copy(sl, r, tok):
        return pltpu.make_async_copy(z_ref.at[pl.ds(tok, 1), :], gbuf.at[sl, pl.ds(r, 1), :], gsem.at[sl])

    def scatter_copy(sl, r, tok):
        return pltpu.make_async_copy(obuf.at[sl, pl.ds(r, 1), :], z_ref.at[pl.ds(tok, 1), pl.ds(d, d)], ssem)

    def wait_scatter(sl):
        pltpu.make_async_copy(obuf.at[sl], z_ref.at[pl.ds(0, cap), pl.ds(d, d)], ssem).wait()

    def wait_gather(sl):
        pltpu.make_async_copy(z_ref.at[pl.ds(0, cap), :], gbuf.at[sl], gsem.at[sl]).wait()

    total = ne * nb
    first, last = n == 0, n == total - 1

    @pl.when(first)
    def _():
        row_copies(0, lambda r, tok: gather_copy(0, r, tok).start())
        obuf[1] = jnp.zeros((cap, d), F32)

    wait_gather(slot)
    hbuf[...] = _rms(gbuf[slot, :, 0:d], gf_ref[...]).astype(BF16)
    obuf[slot] = jnp.zeros((cap, d), F32)
    nf = wg_ref.shape[2] // fchunk
    per = cap // nf

    base_p, tok0_p = rows_of(jnp.maximum(n - 1, 0))
    tok0_p = jnp.where(first, nb * seq, tok0_p)
    base_n, tok0_n = rows_of(jnp.minimum(n + 1, total - 1))
    for i in range(nf):
        fs = slice(i * fchunk, (i + 1) * fchunk)
        h = hbuf[...]
        g = jnp.dot(h, wg_ref[0, :, fs], preferred_element_type=F32)
        up = jnp.dot(h, wu_ref[0, :, fs], preferred_element_type=F32)
        hid = (jax.nn.silu(g) * up).astype(BF16)
        obuf[slot] = obuf[slot] + jnp.dot(hid, wd_ref[0, fs, :], preferred_element_type=F32)
        for r in range(i * per, (i + 1) * per):
            gather_copy(other, r, idx_ref[base_n + r] + tok0_n).start()
            scatter_copy(other, r, idx_ref[base_p + r] + tok0_p).start()
    wait_scatter(other)
    obuf[slot] = gbuf[slot, :, d:2 * d] + obuf[slot] * gate_ref[0, 0]

    @pl.when(last)
    def _():
        wait_gather(other)
        row_copies(n, lambda r, tok: scatter_copy(slot, r, tok).start())
        wait_scatter(slot)


def _moe(idx, gate, z, gf, wg, wu, wd, layer, seq, fchunk=512):
    b, ne, cap, _ = idx.shape
    d, ff = wg.shape[2], wg.shape[3]
    fchunk = min(fchunk, ff)
    assert cap % ROW_UNROLL == 0 and ff % fchunk == 0 and cap % (ff // fchunk) == 0
    assert b >= 3 and z.shape[0] == (b + 1) * seq, "z carries one pad sequence after the real ones"
    once = pl.Buffered(1)
    grid_spec = pltpu.PrefetchScalarGridSpec(
        num_scalar_prefetch=1,
        grid=(ne, b),
        in_specs=[pl.BlockSpec((1, 1, cap, 1), lambda e, bi, idx: (bi, e, 0, 0)),
                  pl.BlockSpec((1, d), lambda e, bi, idx: (0, 0)),
                  pl.BlockSpec((None, 1, d, ff), lambda e, bi, idx: (layer, e, 0, 0), pipeline_mode=once),
                  pl.BlockSpec((None, 1, d, ff), lambda e, bi, idx: (layer, e, 0, 0), pipeline_mode=once),
                  pl.BlockSpec((None, 1, ff, d), lambda e, bi, idx: (layer, e, 0, 0), pipeline_mode=once),
                  pl.BlockSpec(memory_space=pl.ANY)],
        out_specs=pl.BlockSpec(memory_space=pl.ANY),
        scratch_shapes=[pltpu.VMEM((2, cap, 2 * d), F32), pltpu.VMEM((2, cap, d), F32), pltpu.VMEM((cap, d), BF16),
                        pltpu.SemaphoreType.DMA((2,)), pltpu.SemaphoreType.DMA(())],
    )
    return pl.pallas_call(
        functools.partial(_moe_kernel, ne=ne, nb=b, cap=cap, seq=seq, d=d, fchunk=fchunk),
        grid_spec=grid_spec,
        out_shape=jax.ShapeDtypeStruct(z.shape, z.dtype),
        input_output_aliases={6: 0},
        compiler_params=_params(("arbitrary", "arbitrary"), disable_bounds_checks=True),
        name="moe",
    )(idx.reshape(-1), gate, gf, wg, wu, wd, z)


def _tile2(g):
    return jnp.concatenate([g, g], axis=-1)


def kernel(x, attn_norm, w_in, q_norm_a, k_norm_a, q_norm_b, k_norm_b, rel_pos_bias, s5_a_re, s5_a_im, s5_log_dt, s5_b_re, s5_b_im, s5_c_re, s5_c_im, s5_d, w_glu, out_norm_a, out_norm_b, out_norm_c, w_out, ffn_norm, w_router, w_gate, w_up, w_down):
    b, s, d = x.shape
    depth = w_in.shape[0]
    assert b == SUBLANES, "the S5 scan keeps one sequence per sublane"
    tm = min(512, s)
    steps = min(128, s)
    cap = EC_CAPACITY * s // N_EXPERTS
    scale = HEAD_DIM ** -0.5
    nd = len(DILS)

    w_in_b, w_out_b, w_glu_b = w_in.astype(BF16), w_out.astype(BF16), w_glu.astype(BF16)
    w_gate_b, w_up_b, w_down_b = w_gate.astype(BF16), w_up.astype(BF16), w_down.astype(BF16)
    wr_t = jnp.swapaxes(w_router, 1, 2)
    wr_hi = wr_t.astype(BF16)
    wr_lo = (wr_t - wr_hi.astype(F32)).astype(BF16)

    xz, colblk = x.reshape(b * s, d), 0
    for l in range(depth):
        hg = jnp.stack([_tile2(q_norm_a[l]) * scale, _tile2(k_norm_a[l]),
                        _tile2(q_norm_b[l]) * scale, _tile2(k_norm_b[l])], axis=0)
        outs = _in_proj(xz, colblk, b, s, attn_norm[l][None], w_in_b[l], hg, tm)
        qa, ka, va = outs[0:nd], outs[nd:2 * nd], outs[2 * nd:3 * nd]
        qb, kb, vb, u2 = outs[3 * nd:]

        oa = _dilated(qa, ka, va)
        ob = _natten(qb, kb, vb, _na_bias(rel_pos_bias[l]))

        pf = _s5_params(s5_a_re[l, 0], s5_a_im[l, 0], s5_log_dt[l, 0], s5_b_re[l, 0], s5_b_im[l, 0],
                        s5_c_re[l, 0], s5_c_im[l, 0])
        pb = _s5_params(s5_a_re[l, 1], s5_a_im[l, 1], s5_log_dt[l, 1], s5_b_re[l, 1], s5_b_im[l, 1],
                        s5_c_re[l, 1], s5_c_im[l, 1])
        avec = jnp.broadcast_to(jnp.stack([pf[0], pf[1], pb[0], pb[1]])[:, None, :],
                                (4, SUBLANES, pf[0].shape[0]))
        yf, yb = _s5(u2, pf[2], pb[2], pf[3], pb[3], avec, steps)

        z, aff = _out_proj(oa, ob, yf, yb, u2, xz, colblk,
                           out_norm_a[l][None], out_norm_b[l][None], out_norm_c[l][None], s5_d[l][None],
                           w_glu_b[l], w_out_b[l], ffn_norm[l][None], wr_hi[l], wr_lo[l], tm)

        idx, gate = _route(aff[:b], cap)
        xz = _moe(idx, gate, z, ffn_norm[l][None], w_gate_b, w_up_b, w_down_b, l, s)
        colblk = 1
    return xz[:b * s, d:].reshape(b, s, d)
```
